```python
import math
import jax, jax.numpy as jnp
from jax import lax
import numpy as np

D_MODEL = 1024
BATCH = 32
SEQ = 2048
DEPTH = 4
DEC_BATCH = 8
DEC_SEQ = 4096
PAST_LEN = 128

HEAD_DIM = 64
CHUNK = 128
A_GROUPS = 8
W_A = A_GROUPS * HEAD_DIM
DILATED_GROUPS = ((128, 1), (512, 4), (2048, 16))
HEADS_PER_GROUP = 2
H_B = HEADS_PER_GROUP * len(DILATED_GROUPS)
W_B = H_B * HEAD_DIM
H_C = 4
W_C_QK = 2 * H_C * HEAD_DIM
W_C = H_C * 2 * HEAD_DIM
N_ATTN_HEADS = H_B + H_C
NUM_BUCKETS = 32
REL_MAX_DIST = 1024
QBLK = 128
D_FF = ((8 * D_MODEL + 3 * 256 - 1) // (3 * 256)) * 256
N_BRANCH = 3
SPLIT_SIZES = (2 * W_A, 3 * W_B, 2 * W_C_QK + W_C, N_BRANCH * D_MODEL)
IN_COLS = sum(SPLIT_SIZES)
SPLIT_POINTS = tuple(int(c) for c in np.cumsum(SPLIT_SIZES)[:-1])
RMS_EPS = 1e-6
LN_EPS = 1e-5
NEG_INF = -1e30

kernel_name = 'hybrid_gated_gmlp_dilated_diff_encoder'


def rms_norm(x, g):
    xf = x.astype(jnp.float32)
    y = xf * lax.rsqrt(jnp.mean(xf * xf, axis=-1, keepdims=True) + RMS_EPS)
    return (y * g.astype(jnp.float32)).astype(x.dtype)


def layer_norm(x, g, b):
    xf = x.astype(jnp.float32)
    xc = xf - jnp.mean(xf, axis=-1, keepdims=True)
    var = jnp.mean(xc * xc, axis=-1, keepdims=True)
    return (xc * lax.rsqrt(var + LN_EPS) * g.astype(jnp.float32) + b.astype(jnp.float32)).astype(x.dtype)


def rel_bucket(rel):
    nb = NUM_BUCKETS // 2
    max_exact = nb // 2
    n = jnp.abs(rel)
    sign_off = jnp.where(rel > 0, nb, 0)
    nf = jnp.maximum(n, 1).astype(jnp.float32)
    large = max_exact + (jnp.log(nf / max_exact) / math.log(REL_MAX_DIST / max_exact)
                         * (nb - max_exact)).astype(jnp.int32)
    large = jnp.minimum(large, nb - 1)
    return sign_off + jnp.where(n < max_exact, n, large)


def banded_attention(q, k, v, bias, half):
    n, L, h, hd = q.shape
    nb = -(-L // half)
    lp = nb * half
    pad = lp - L
    qp = jnp.pad(q, ((0, 0), (0, pad), (0, 0), (0, 0))).reshape(n, nb, half, h, hd)

    def windows(t):
        tp = jnp.pad(t, ((0, 0), (half, pad + half), (0, 0), (0, 0))).reshape(n, nb + 2, half, h, t.shape[-1])
        return jnp.concatenate([tp[:, :-2], tp[:, 1:-1], tp[:, 2:]], axis=2)

    kw = windows(k)
    vw = windows(v)
    s = jnp.einsum('nbqhd,nbkhd->nbhqk', qp, kw).astype(jnp.float32) + bias.astype(jnp.float32)[None, None]
    qpos = jnp.arange(nb)[:, None] * half + jnp.arange(half)[None, :]
    kpos = (jnp.arange(nb)[:, None] - 1) * half + jnp.arange(3 * half)[None, :]
    kp3 = kpos[:, None, :]
    valid = (kp3 >= 0) & (kp3 < L) & (jnp.abs(kp3 - qpos[:, :, None]) <= half)
    s = jnp.where(valid[None, :, None], s, NEG_INF)
    lse = jax.nn.logsumexp(s, axis=-1)
    p = jnp.exp(s - lse[..., None])
    o = jnp.einsum('nbhqk,nbkhd->nbqhd', p.astype(v.dtype), vw).reshape(n, lp, h, v.shape[-1])[:, :L]
    lse = jnp.swapaxes(lse, 2, 3).reshape(n, lp, h)[:, :L]
    return o, lse


def dilated_attention(q, k, v, table, window, dil):
    b, s, h, hd = q.shape
    half = window // (2 * dil)
    sub = s // dil

    def to_sub(t):
        return jnp.swapaxes(t.reshape(b, sub, dil, h, t.shape[-1]), 1, 2).reshape(b * dil, sub, h, t.shape[-1])

    rel = (jnp.arange(3 * half)[None, :] - half - jnp.arange(half)[:, None]) * dil
    bias = jnp.transpose(table[rel_bucket(rel)], (2, 0, 1))
    o, lse = banded_attention(to_sub(q), to_sub(k), to_sub(v), bias, half)
    o = jnp.swapaxes(o.reshape(b, dil, sub, h, hd), 1, 2).reshape(b, s, h, hd)
    lse = jnp.swapaxes(lse.reshape(b, dil, sub, h), 1, 2).reshape(b, s, h)
    return o, lse


def diff_attention(q, k, v, table, lam):
    b, s, _, h, hd = q.shape
    nq = s // QBLK
    q_blocks = jnp.moveaxis(q.reshape(b, nq, QBLK, 2, h, hd), 1, 0)
    starts = jnp.arange(nq, dtype=jnp.int32) * QBLK
    kpos = jnp.arange(s, dtype=jnp.int32)

    def one_block(args):
        qi, q0 = args
        logits = jnp.einsum('bqchd,bkchd->bchqk', qi, k).astype(jnp.float32)
        rel = kpos[None, :] - (q0 + jnp.arange(QBLK, dtype=jnp.int32))[:, None]
        bias = jnp.transpose(table[rel_bucket(rel)], (2, 0, 1)).astype(jnp.float32)
        probs = jax.nn.softmax(logits + bias[None, None], axis=-1)
        attn = probs[:, 0] - lam * probs[:, 1]
        return jnp.einsum('bhqk,bkhe->bqhe', attn.astype(v.dtype), v)

    o = lax.map(one_block, (q_blocks, starts))
    return jnp.moveaxis(o, 0, 1).reshape(b, s, h, v.shape[-1])


def encoder_layer(x, p, l):
    b, s, _ = x.shape
    lam_init = 0.8 - 0.6 * math.exp(-0.3 * l)
    h = rms_norm(x, p['norm1_g'][l])
    z = h @ p['w_in'][l]
    za, zb, zc, zg = jnp.split(z, SPLIT_POINTS, axis=-1)

    u, v = jnp.split(jax.nn.gelu(za), 2, axis=-1)
    vn = layer_norm(v, p['sgu_ln_g'][l], p['sgu_ln_b'][l]).reshape(b, s // CHUNK, CHUNK, A_GROUPS, HEAD_DIM)
    sv = jnp.einsum('gpq,bcqgd->bcpgd', p['sgu_w'][l], vn) + p['sgu_b'][l].T[None, None, :, :, None]
    a_out = u * sv.reshape(b, s, W_A)

    qb, kb, vb = (t.reshape(b, s, H_B, HEAD_DIM) for t in jnp.split(zb, 3, axis=-1))
    qb = rms_norm(qb, p['qn_b'][l]) * HEAD_DIM ** -0.5
    kb = rms_norm(kb, p['kn_b'][l])
    outs, lses = [], []
    for gi, (window, dil) in enumerate(DILATED_GROUPS):
        hs = slice(gi * HEADS_PER_GROUP, (gi + 1) * HEADS_PER_GROUP)
        o_g, lse_g = dilated_attention(qb[:, :, hs], kb[:, :, hs], vb[:, :, hs], p['rel_bias'][:, hs], window, dil)
        outs.append(o_g)
        lses.append(lse_g)
    o_b = jnp.stack(outs, axis=2)
    alpha = jax.nn.softmax(jnp.stack(lses, axis=2), axis=2)
    b_out = (o_b * alpha[..., None].astype(o_b.dtype)).reshape(b, s, W_B)

    qc, kc, vc = jnp.split(zc, (W_C_QK, 2 * W_C_QK), axis=-1)
    qc = rms_norm(qc.reshape(b, s, 2, H_C, HEAD_DIM), p['qn_c'][l]) * HEAD_DIM ** -0.5
    kc = rms_norm(kc.reshape(b, s, 2, H_C, HEAD_DIM), p['kn_c'][l])
    vc = vc.reshape(b, s, H_C, 2 * HEAD_DIM)
    lq1 = p['lam_q1'][l].astype(jnp.float32)
    lk1 = p['lam_k1'][l].astype(jnp.float32)
    lq2 = p['lam_q2'][l].astype(jnp.float32)
    lk2 = p['lam_k2'][l].astype(jnp.float32)
    lam = jnp.exp(jnp.sum(lq1 * lk1)) - jnp.exp(jnp.sum(lq2 * lk2)) + lam_init
    o_c = diff_attention(qc, kc, vc, p['rel_bias'][:, H_B:], lam)
    c_out = (rms_norm(o_c, p['subln_g'][l]) * (1.0 - lam_init)).reshape(b, s, W_C)

    gates = jax.nn.sigmoid(zg.astype(jnp.float32)).astype(x.dtype).reshape(b, s, N_BRANCH, D_MODEL)
    merged = (gates[:, :, 0] * (a_out @ p['w_pa'][l])
              + gates[:, :, 1] * (b_out @ p['w_pb'][l])
              + gates[:, :, 2] * (c_out @ p['w_pc'][l]))
    x = x + merged @ p['w_o'][l]

    h2 = rms_norm(x, p['norm2_g'][l])
    g_ff, u_ff = jnp.split(h2 @ p['w_gu'][l], 2, axis=-1)
    return x + (jax.nn.silu(g_ff) * u_ff) @ p['w_down'][l]


def encoder_trunk(x, p):
    for l in range(DEPTH):
        x = encoder_layer(x, p, l)
    return x


def setup_inputs(seed: int = 0) -> dict:
    key = jax.random.key(seed)
    ks = jax.random.split(key, 32)

    def nrm(k, shape, scale):
        return jax.random.normal(k, shape, jnp.float32) * scale

    def gain(k, shape):
        return 1.0 + 0.05 * jax.random.normal(k, shape, jnp.float32)

    return {
        'x_prompt': nrm(ks[0], (BATCH, SEQ, D_MODEL), 1.0),
        'x_sample': nrm(ks[1], (DEC_BATCH, DEC_SEQ, D_MODEL), 1.0),
        'rel_bias': nrm(ks[2], (NUM_BUCKETS, N_ATTN_HEADS), 0.5),
        'norm1_g': gain(ks[3], (DEPTH, D_MODEL)),
        'w_in': nrm(ks[4], (DEPTH, D_MODEL, IN_COLS), D_MODEL ** -0.5),
        'sgu_ln_g': gain(ks[5], (DEPTH, W_A)),
        'sgu_ln_b': nrm(ks[6], (DEPTH, W_A), 0.02),
        'sgu_w': nrm(ks[7], (DEPTH, A_GROUPS, CHUNK, CHUNK), CHUNK ** -0.5),
        'sgu_b': 1.0 + nrm(ks[8], (DEPTH, A_GROUPS, CHUNK), 0.1),
        'qn_b': gain(ks[9], (DEPTH, HEAD_DIM)),
        'kn_b': gain(ks[10], (DEPTH, HEAD_DIM)),
        'qn_c': gain(ks[11], (DEPTH, HEAD_DIM)),
        'kn_c': gain(ks[12], (DEPTH, HEAD_DIM)),
        'lam_q1': nrm(ks[13], (DEPTH, HEAD_DIM), 0.1),
        'lam_k1': nrm(ks[14], (DEPTH, HEAD_DIM), 0.1),
        'lam_q2': nrm(ks[15], (DEPTH, HEAD_DIM), 0.1),
        'lam_k2': nrm(ks[16], (DEPTH, HEAD_DIM), 0.1),
        'subln_g': gain(ks[17], (DEPTH, 2 * HEAD_DIM)),
        'w_pa': nrm(ks[18], (DEPTH, W_A, D_MODEL), W_A ** -0.5),
        'w_pb': nrm(ks[19], (DEPTH, W_B, D_MODEL), W_B ** -0.5),
        'w_pc': nrm(ks[20], (DEPTH, W_C, D_MODEL), W_C ** -0.5),
        'w_o': nrm(ks[21], (DEPTH, D_MODEL, D_MODEL), D_MODEL ** -0.5),
        'norm2_g': gain(ks[22], (DEPTH, D_MODEL)),
        'w_gu': nrm(ks[23], (DEPTH, D_MODEL, 2 * D_FF), D_MODEL ** -0.5),
        'w_down': nrm(ks[24], (DEPTH, D_FF, D_MODEL), D_FF ** -0.5),
    }


def reference(x_prompt, x_sample, rel_bias, norm1_g, w_in, sgu_ln_g, sgu_ln_b, sgu_w, sgu_b,
              qn_b, kn_b, qn_c, kn_c, lam_q1, lam_k1, lam_q2, lam_k2, subln_g,
              w_pa, w_pb, w_pc, w_o, norm2_g, w_gu, w_down):
    params = dict(rel_bias=rel_bias, norm1_g=norm1_g, w_in=w_in, sgu_ln_g=sgu_ln_g, sgu_ln_b=sgu_ln_b,
                  sgu_w=sgu_w, sgu_b=sgu_b, qn_b=qn_b, kn_b=kn_b, qn_c=qn_c, kn_c=kn_c,
                  lam_q1=lam_q1, lam_k1=lam_k1, lam_q2=lam_q2, lam_k2=lam_k2, subln_g=subln_g,
                  w_pa=w_pa, w_pb=w_pb, w_pc=w_pc, w_o=w_o, norm2_g=norm2_g, w_gu=w_gu, w_down=w_down)
    y_prompt = encoder_trunk(x_prompt, params)
    y_sample = encoder_trunk(x_sample, params)
    return (y_prompt, y_sample)
```

```python
import functools
import math

import numpy as np
import jax
import jax.numpy as jnp
from jax import lax
from jax.experimental import pallas as pl
from jax.experimental.pallas import tpu as pltpu

D_MODEL = 1024
HEAD_DIM = 64
LANES = 128
CHUNK = 128
A_GROUPS = 8
W_A = A_GROUPS * HEAD_DIM
DILATED_GROUPS = ((128, 1), (512, 4), (2048, 16))
HEADS_PER_GROUP = 2
N_GROUPS = len(DILATED_GROUPS)
H_B = HEADS_PER_GROUP * N_GROUPS
W_B = H_B * HEAD_DIM
H_C = 4
W_C_QK = 2 * H_C * HEAD_DIM
W_C = H_C * 2 * HEAD_DIM
NUM_BUCKETS = 32
REL_MAX_DIST = 1024
D_FF = ((8 * D_MODEL + 3 * 256 - 1) // (3 * 256)) * 256
N_BRANCH = 3
RMS_EPS = 1e-6
LN_EPS = 1e-5
NEG_INF = -1e30
BAND = 64
assert all(w // (2 * d) == BAND for w, d in DILATED_GROUPS)
assert HEADS_PER_GROUP * HEAD_DIM == LANES

OFF_A = 0
OFF_B = 2 * W_A
OFF_C = OFF_B + 3 * W_B
OFF_G = OFF_C + 2 * W_C_QK + W_C
IN_COLS = OFF_G + N_BRANCH * D_MODEL
GRP_COLS = 3 * LANES

FF_CHUNK = 256
VMEM_LIMIT = 56 * 2 ** 20

_F32 = jnp.float32
_BF16 = jnp.bfloat16


def _in_proj_permutation():
    cols = list(range(OFF_B))
    for g in range(N_GROUPS):
        for part in range(3):
            base = OFF_B + part * W_B + g * LANES
            cols += list(range(base, base + LANES))
    for part in range(2):
        for h in range(H_C):
            for c in range(2):
                base = OFF_C + part * W_C_QK + c * (H_C * HEAD_DIM) + h * HEAD_DIM
                cols += list(range(base, base + HEAD_DIM))
    cols += list(range(OFF_C + 2 * W_C_QK, IN_COLS))
    assert sorted(cols) == list(range(IN_COLS))
    return np.asarray(cols, np.int32)


def _bucket_thresholds(max_dist):
    nb = NUM_BUCKETS // 2
    max_exact = nb // 2
    n = np.arange(1, max_dist + 1)

    def idx(dtype):
        nf = n.astype(dtype)
        large = max_exact + (np.log(nf / dtype(max_exact)) / dtype(math.log(REL_MAX_DIST / max_exact))
                             * dtype(nb - max_exact)).astype(np.int32)
        return np.where(n < max_exact, n, np.minimum(large, nb - 1))

    i32, i64 = idx(np.float32), idx(np.float64)
    assert np.array_equal(i32, i64) and np.all(np.diff(i32) >= 0) and np.all(np.diff(i32) <= 1)
    thr = [0] + [int(n[np.argmax(i32 >= b)]) for b in range(1, nb)]
    return thr


_THR = _bucket_thresholds(1 << 16)


def _bias_from_rel(rel, tab_ref, head):
    nb = NUM_BUCKETS // 2
    val = jnp.full(rel.shape, tab_ref[nb - 1, head], _F32)
    for b in range(nb - 2, -1, -1):
        val = jnp.where(rel >= -(_THR[b + 1] - 1), tab_ref[b, head], val)
    for b in range(1, nb):
        val = jnp.where(rel >= _THR[b], tab_ref[nb + b, head], val)
    return val


def _bias_c_kernel(tab_ref, out_ref, *, tr):
    h = pl.program_id(0)
    i = pl.program_id(1)
    s = out_ref.shape[-1]
    qpos = lax.broadcasted_iota(jnp.int32, (tr, s), 0) + i * tr
    kpos = lax.broadcasted_iota(jnp.int32, (tr, s), 1)
    out_ref[0] = _bias_from_rel(kpos - qpos, tab_ref, H_B + h)


def _bias_c(rel_bias, s):
    tr = min(128, s)
    return pl.pallas_call(
        functools.partial(_bias_c_kernel, tr=tr),
        grid=(H_C, s // tr),
        in_specs=[pl.BlockSpec(memory_space=pltpu.SMEM)],
        out_specs=pl.BlockSpec((1, tr, s), lambda h, i: (h, i, 0)),
        out_shape=jax.ShapeDtypeStruct((H_C, s, s), _F32),
        name="bias_c",
    )(rel_bias)


def _band_tiling(length):
    tq = min(256, length)
    kw = min(tq + 2 * BAND, length)
    nt = length // tq
    assert nt * tq == length
    starts = [min(max(t * tq - BAND, 0), length - kw) for t in range(nt)]
    offs = sorted({starts[t] - t * tq for t in range(nt)})
    variants = [offs.index(starts[t] - t * tq) for t in range(nt)]
    return tq, kw, starts, offs, variants


def _bias_b_kernel(tab_ref, out_ref, *, group, dil, offs):
    _, _, tq, kw = out_ref.shape
    i = lax.broadcasted_iota(jnp.int32, (tq, kw), 0)
    j = lax.broadcasted_iota(jnp.int32, (tq, kw), 1)
    for v, off in enumerate(offs):
        dist = j + off - i
        inside = jnp.abs(dist) <= BAND
        for hh in range(HEADS_PER_GROUP):
            val = _bias_from_rel(dist * dil, tab_ref, group * HEADS_PER_GROUP + hh)
            out_ref[hh, v] = jnp.where(inside, val, NEG_INF)


def _bias_b(rel_bias, group, dil, length):
    tq, kw, _, offs, _ = _band_tiling(length)
    return pl.pallas_call(
        functools.partial(_bias_b_kernel, group=group, dil=dil, offs=offs),
        in_specs=[pl.BlockSpec(memory_space=pltpu.SMEM)],
        out_shape=jax.ShapeDtypeStruct((HEADS_PER_GROUP, len(offs), tq, kw), _F32),
        name=f"bias_b{group}",
    )(rel_bias)


def _head_rms(z, gain):
    lane = lax.broadcasted_iota(jnp.int32, (1, LANES), 1)
    first = lane < HEAD_DIM
    sq = z * z
    s0 = jnp.sum(jnp.where(first, sq, 0.0), axis=-1, keepdims=True)
    s1 = jnp.sum(jnp.where(first, 0.0, sq), axis=-1, keepdims=True)
    ms = jnp.where(first, s0, s1) * (1.0 / HEAD_DIM)
    return (z * lax.rsqrt(ms + RMS_EPS)) * gain


def _sigmoid(x):
    return 1.0 / (1.0 + jnp.exp(-x))


def _gelu_tanh(x):
    c = math.sqrt(2.0 / math.pi)
    return x * (0.5 * (1.0 + jnp.tanh(c * (x + 0.044715 * (x * x * x)))))


def _in_proj_kernel(l_ref, x_ref, n1g_ref, win_ref, lng_ref, lnb_ref, sguw_ref, sgub_ref,
                    qgb_ref, kgb_ref, qgc_ref, kgc_ref, wpa_ref,
                    ma_ref, g12_ref, zb0_ref, zb1_ref, zb2_ref, qc_ref, kc_ref, vc_ref, a_scr):
    del l_ref
    tm = x_ref.shape[0]
    x = x_ref[...]
    h = ((x * lax.rsqrt(jnp.mean(x * x, axis=-1, keepdims=True) + RMS_EPS)) * n1g_ref[0]).astype(_BF16)

    def proj(lo, hi):
        return jnp.dot(h, win_ref[0, :, lo:hi], preferred_element_type=_F32)

    ga = _gelu_tanh(proj(OFF_A, OFF_B))
    u, v = ga[:, :W_A], ga[:, W_A:]
    vc = v - jnp.mean(v, axis=-1, keepdims=True)
    var = jnp.mean(vc * vc, axis=-1, keepdims=True)
    vn = ((vc * lax.rsqrt(var + LN_EPS)) * lng_ref[0] + lnb_ref[0]).astype(_BF16)
    first = lax.broadcasted_iota(jnp.int32, (1, LANES), 1) < HEAD_DIM
    for c in range(tm // CHUNK):
        rows = slice(c * CHUNK, (c + 1) * CHUNK)
        for j in range(A_GROUPS // 2):
            cols = slice(j * LANES, (j + 1) * LANES)
            slab = vn[rows, cols]
            r0 = jnp.dot(sguw_ref[0, 2 * j], slab, preferred_element_type=_F32)
            r1 = jnp.dot(sguw_ref[0, 2 * j + 1], slab, preferred_element_type=_F32)
            sv = jnp.where(first, r0, r1) + sgub_ref[0, :, cols]
            a_scr[rows, cols] = (u[rows, cols] * sv).astype(_BF16)

    gates = _sigmoid(proj(OFF_G, IN_COLS))
    ma_ref[...] = gates[:, :D_MODEL] * jnp.dot(a_scr[...], wpa_ref[0], preferred_element_type=_F32)
    g12_ref[...] = gates[:, D_MODEL:]

    zb = proj(OFF_B, OFF_C)
    for g, out in enumerate((zb0_ref, zb1_ref, zb2_ref)):
        base = g * GRP_COLS
        out[:, 0:LANES] = (_head_rms(zb[:, base:base + LANES], qgb_ref[0]) * HEAD_DIM ** -0.5).astype(_BF16)
        out[:, LANES:2 * LANES] = _head_rms(zb[:, base + LANES:base + 2 * LANES], kgb_ref[0]).astype(_BF16)
        out[:, 2 * LANES:] = zb[:, base + 2 * LANES:base + 3 * LANES].astype(_BF16)

    zc = proj(OFF_C, OFF_G)
    for hh in range(H_C):
        cols = slice(hh * LANES, (hh + 1) * LANES)
        qc_ref[:, cols] = (_head_rms(zc[:, cols], qgc_ref[0]) * HEAD_DIM ** -0.5).astype(_BF16)
        kc_ref[:, cols] = _head_rms(zc[:, W_C_QK + hh * LANES:W_C_QK + (hh + 1) * LANES],
                                    kgc_ref[0]).astype(_BF16)
    vc_ref[...] = zc[:, 2 * W_C_QK:].astype(_BF16)


def _layer_spec(shape):
    nd = len(shape)
    return pl.BlockSpec((1,) + tuple(shape[1:]), lambda i, l: (l[0],) + (0,) * (nd - 1),
                        pipeline_mode=pl.Buffered(1))


def _in_proj(l, x, p):
    t = x.shape[0]
    tm = min(512, t)
    row = lambda w: pl.BlockSpec((tm, w), lambda i, l: (i, 0))
    weights = (p["norm1_g"], p["w_in"], p["sgu_ln_g"], p["sgu_ln_b"], p["sgu_w"], p["sgu_b"],
               p["qn_b"], p["kn_b"], p["qn_c"], p["kn_c"], p["w_pa"])
    out_widths = (D_MODEL, 2 * D_MODEL, GRP_COLS, GRP_COLS, GRP_COLS, W_C_QK, W_C_QK, W_C)
    out_dtypes = (_F32, _F32) + (_BF16,) * 6
    return pl.pallas_call(
        _in_proj_kernel,
        grid_spec=pltpu.PrefetchScalarGridSpec(
            num_scalar_prefetch=1,
            grid=(t // tm,),
            in_specs=[row(D_MODEL)] + [_layer_spec(w.shape) for w in weights],
            out_specs=[row(w) for w in out_widths],
            scratch_shapes=[pltpu.VMEM((tm, W_A), _BF16)],
        ),
        out_shape=[jax.ShapeDtypeStruct((t, w), dt) for w, dt in zip(out_widths, out_dtypes)],
        compiler_params=pltpu.CompilerParams(dimension_semantics=("parallel",),
                                             vmem_limit_bytes=VMEM_LIMIT),
        name="in_proj",
    )(l, x, *weights)


def _dilated_kernel(zb_ref, bias_ref, o_ref, lse_ref, *, dil, tq, kw, starts, variants):
    lane = lax.broadcasted_iota(jnp.int32, (1, LANES), 1)
    first = lane < HEAD_DIM
    head_mask = (first.astype(_BF16), (~first).astype(_BF16))
    for r in range(dil):
        c0 = r * GRP_COLS
        for t, (ks, var) in enumerate(zip(starts, variants)):
            q0 = t * tq
            q = zb_ref[0, q0:q0 + tq, c0:c0 + LANES]
            k = zb_ref[0, ks:ks + kw, c0 + LANES:c0 + 2 * LANES]
            v = zb_ref[0, ks:ks + kw, c0 + 2 * LANES:c0 + 3 * LANES]
            outs, lses = [], []
            for hh in range(HEADS_PER_GROUP):
                s = lax.dot_general(q * head_mask[hh], k, (((1,), (1,)), ((), ())),
                                    preferred_element_type=_F32) + bias_ref[hh, var]
                m = jnp.max(s, axis=-1, keepdims=True)
                e = jnp.exp(s - m)
                den = jnp.sum(e, axis=-1, keepdims=True)
                outs.append(jnp.dot(e.astype(_BF16), v, preferred_element_type=_F32) * (1.0 / den))
                lses.append(m + jnp.log(den))
            cols = slice(r * LANES, (r + 1) * LANES)
            o_ref[0, q0:q0 + tq, cols] = jnp.where(first, outs[0], outs[1])
            lse_ref[0, q0:q0 + tq, cols] = jnp.where(first, lses[0], lses[1])


def _dilated_attention(zb, bias, dil, batch, seq):
    length = seq // dil
    tq, kw, starts, _, variants = _band_tiling(length)
    zb = zb.reshape(batch, length, dil * GRP_COLS)
    out = jax.ShapeDtypeStruct((batch, length, dil * LANES), _F32)
    o, lse = pl.pallas_call(
        functools.partial(_dilated_kernel, dil=dil, tq=tq, kw=kw, starts=starts, variants=variants),
        grid=(batch,),
        in_specs=[pl.BlockSpec((1, length, dil * GRP_COLS), lambda b: (b, 0, 0)),
                  pl.BlockSpec(bias.shape, lambda b: (0, 0, 0, 0))],
        out_specs=[pl.BlockSpec((1, length, dil * LANES), lambda b: (b, 0, 0))] * 2,
        out_shape=[out, out],
        compiler_params=pltpu.CompilerParams(dimension_semantics=("parallel",),
                                             vmem_limit_bytes=VMEM_LIMIT),
        name=f"dilated_d{dil}",
    )(zb, bias)
    return o.reshape(batch * seq, LANES), lse.reshape(batch * seq, LANES)


def _diff_kernel(l_ref, lam_init_ref, q_ref, k_ref, v_ref, bias_ref, lq1_ref, lk1_ref, lq2_ref, lk2_ref,
                 sg_ref, o_ref):
    lam_init = lam_init_ref[l_ref[0]]
    lam = (jnp.exp(jnp.sum(lq1_ref[0] * lk1_ref[0], axis=-1, keepdims=True))
           - jnp.exp(jnp.sum(lq2_ref[0] * lk2_ref[0], axis=-1, keepdims=True)) + lam_init)
    lane = lax.broadcasted_iota(jnp.int32, (1, LANES), 1)
    first = lane < HEAD_DIM
    q = q_ref[0]
    k = k_ref[0]
    v = v_ref[0]
    bias = bias_ref[0]
    outs = []
    for mask in (first.astype(_BF16), (~first).astype(_BF16)):
        s = lax.dot_general(q * mask, k, (((1,), (1,)), ((), ())), preferred_element_type=_F32) + bias
        m = jnp.max(s, axis=-1, keepdims=True)
        e = jnp.exp(s - m)
        den = jnp.sum(e, axis=-1, keepdims=True)
        outs.append(jnp.dot(e.astype(_BF16), v, preferred_element_type=_F32) * (1.0 / den))
    o = outs[0] - lam * outs[1]
    y = (o * lax.rsqrt(jnp.mean(o * o, axis=-1, keepdims=True) + RMS_EPS)) * sg_ref[0]
    o_ref[0] = (y * (1.0 - lam_init)).astype(o_ref.dtype)


def _diff_attention(l, lam_init, qc, kc, vc, bias, p, batch, seq):
    tq = min(256, seq)
    qc, kc, vc = (a.reshape(batch, seq, a.shape[-1]) for a in (qc, kc, vc))
    q_spec = pl.BlockSpec((1, tq, LANES), lambda h, i, b, l: (b, i, h))
    kv_spec = pl.BlockSpec((1, seq, LANES), lambda h, i, b, l: (b, 0, h))
    vec = lambda w: pl.BlockSpec((1, 1, w), lambda h, i, b, l: (l[0], 0, 0))
    out = pl.pallas_call(
        _diff_kernel,
        grid_spec=pltpu.PrefetchScalarGridSpec(
            num_scalar_prefetch=1,
            grid=(H_C, seq // tq, batch),
            in_specs=[pl.BlockSpec(memory_space=pltpu.SMEM), q_spec, kv_spec, kv_spec,
                      pl.BlockSpec((1, tq, seq), lambda h, i, b, l: (h, i, 0)),
                      vec(HEAD_DIM), vec(HEAD_DIM), vec(HEAD_DIM), vec(HEAD_DIM), vec(LANES)],
            out_specs=q_spec,
        ),
        out_shape=jax.ShapeDtypeStruct((batch, seq, W_C), _BF16),
        compiler_params=pltpu.CompilerParams(dimension_semantics=("parallel", "parallel", "parallel"),
                                             vmem_limit_bytes=VMEM_LIMIT),
        name="diff_attn",
    )(l, lam_init, qc, kc, vc, bias, p["lam_q1"], p["lam_k1"], p["lam_q2"], p["lam_k2"], p["subln_g"])
    return out.reshape(batch * seq, W_C)


def _merge_ffn_kernel(l_ref, x_ref, ma_ref, g12_ref, o0_ref, o1_ref, o2_ref, s0_ref, s1_ref, s2_ref, c_ref,
                      wpb_ref, wpc_ref, wo_ref, n2g_ref, wgu_ref, wdn_ref, out_ref):
    del l_ref
    lses = (s0_ref[...], s1_ref[...], s2_ref[...])
    m = jnp.maximum(jnp.maximum(lses[0], lses[1]), lses[2])
    es = [jnp.exp(s - m) for s in lses]
    inv = 1.0 / (es[0] + es[1] + es[2])
    b_out = jnp.concatenate([o[...] * (e * inv) for o, e in zip((o0_ref, o1_ref, o2_ref), es)], axis=-1)
    pb = jnp.dot(b_out.astype(_BF16), wpb_ref[0], preferred_element_type=_F32)
    pc = jnp.dot(c_ref[...], wpc_ref[0], preferred_element_type=_F32)
    g12 = g12_ref[...]
    merged = ma_ref[...] + g12[:, :D_MODEL] * pb + g12[:, D_MODEL:] * pc
    x1 = x_ref[...] + jnp.dot(merged.astype(_BF16), wo_ref[0], preferred_element_type=_F32)

    h2 = ((x1 * lax.rsqrt(jnp.mean(x1 * x1, axis=-1, keepdims=True) + RMS_EPS)) * n2g_ref[0]).astype(_BF16)
    acc = x1
    for j in range(D_FF // FF_CHUNK):
        lo = j * FF_CHUNK
        g = jnp.dot(h2, wgu_ref[0, :, lo:lo + FF_CHUNK], preferred_element_type=_F32)
        u = jnp.dot(h2, wgu_ref[0, :, D_FF + lo:D_FF + lo + FF_CHUNK], preferred_element_type=_F32)
        act = ((g * _sigmoid(g)) * u).astype(_BF16)
        acc = acc + jnp.dot(act, wdn_ref[0, lo:lo + FF_CHUNK, :], preferred_element_type=_F32)
    out_ref[...] = acc


def _merge_ffn(l, x, ma, g12, o_groups, lse_groups, c_out, p):
    t = x.shape[0]
    tm = min(256, t)
    row = lambda w: pl.BlockSpec((tm, w), lambda i, l: (i, 0))
    weights = (p["w_pb"], p["w_pc"], p["w_o"], p["norm2_g"], p["w_gu"], p["w_down"])
    acts = (x, ma, g12) + tuple(o_groups) + tuple(lse_groups) + (c_out,)
    return pl.pallas_call(
        _merge_ffn_kernel,
        grid_spec=pltpu.PrefetchScalarGridSpec(
            num_scalar_prefetch=1,
            grid=(t // tm,),
            in_specs=[row(a.shape[1]) for a in acts] + [_layer_spec(w.shape) for w in weights],
            out_specs=row(D_MODEL),
        ),
        out_shape=jax.ShapeDtypeStruct((t, D_MODEL), _F32),
        compiler_params=pltpu.CompilerParams(dimension_semantics=("parallel",),
                                             vmem_limit_bytes=VMEM_LIMIT),
        name="merge_ffn",
    )(l, *acts, *weights)


def _layer(l, x, p, biases, lam_init, batch, seq):
    ma, g12, zb0, zb1, zb2, qc, kc, vc = _in_proj(l, x, p)
    o_groups, lse_groups = [], []
    for g, (zb, (_, dil)) in enumerate(zip((zb0, zb1, zb2), DILATED_GROUPS)):
        o, lse = _dilated_attention(zb, biases["b"][g], dil, batch, seq)
        o_groups.append(o)
        lse_groups.append(lse)
    c_out = _diff_attention(l, lam_init, qc, kc, vc, biases["c"], p, batch, seq)
    return _merge_ffn(l, x, ma, g12, o_groups, lse_groups, c_out, p)


def _prepare_params(p):
    depth = p["w_in"].shape[0]
    row = lambda a: a.reshape(depth, 1, a.shape[-1])
    tile2 = lambda a: row(jnp.tile(a, (1, LANES // HEAD_DIM)))
    sgu_b = jnp.broadcast_to(jnp.swapaxes(p["sgu_b"], 1, 2)[..., None], (depth, CHUNK, A_GROUPS, HEAD_DIM))
    return dict(
        norm1_g=row(p["norm1_g"]), norm2_g=row(p["norm2_g"]),
        w_in=jnp.take(p["w_in"], _in_proj_permutation(), axis=2).astype(_BF16),
        sgu_ln_g=row(p["sgu_ln_g"]), sgu_ln_b=row(p["sgu_ln_b"]),
        sgu_w=p["sgu_w"].astype(_BF16), sgu_b=sgu_b.reshape(depth, CHUNK, W_A),
        qn_b=tile2(p["qn_b"]), kn_b=tile2(p["kn_b"]), qn_c=tile2(p["qn_c"]), kn_c=tile2(p["kn_c"]),
        lam_q1=row(p["lam_q1"]), lam_k1=row(p["lam_k1"]), lam_q2=row(p["lam_q2"]), lam_k2=row(p["lam_k2"]),
        subln_g=row(p["subln_g"]),
        w_pa=p["w_pa"].astype(_BF16), w_pb=p["w_pb"].astype(_BF16), w_pc=p["w_pc"].astype(_BF16),
        w_o=p["w_o"].astype(_BF16), w_gu=p["w_gu"].astype(_BF16), w_down=p["w_down"].astype(_BF16),
    )


def _biases(rel_bias, seq):
    return dict(c=_bias_c(rel_bias, seq),
                b=[_bias_b(rel_bias, g, dil, seq // dil) for g, (_, dil) in enumerate(DILATED_GROUPS)])


@jax.jit
def _trunk(x_prompt, x_sample, params):
    p = _prepare_params(params)
    depth = params["w_in"].shape[0]
    lam_init = jnp.asarray([0.8 - 0.6 * math.exp(-0.3 * l) for l in range(depth)], _F32)
    groups = [(x.shape[0], x.shape[1]) for x in (x_prompt, x_sample)]
    biases = [_biases(params["rel_bias"], seq) for _, seq in groups]

    def body(l, xs):
        lidx = jnp.reshape(l, (1,)).astype(jnp.int32)
        return tuple(_layer(lidx, x, p, b, lam_init, batch, seq)
                     for x, b, (batch, seq) in zip(xs, biases, groups))

    xs = tuple(x.reshape(-1, D_MODEL) for x in (x_prompt, x_sample))
    ys = lax.fori_loop(0, depth, body, xs)
    return tuple(y.reshape(x.shape) for y, x in zip(ys, (x_prompt, x_sample)))


def kernel(x_prompt, x_sample, rel_bias, norm1_g, w_in, sgu_ln_g, sgu_ln_b, sgu_w, sgu_b, qn_b, kn_b, qn_c, kn_c, lam_q1, lam_k1, lam_q2, lam_k2, subln_g, w_pa, w_pb, w_pc, w_o, norm2_g, w_gu, w_down):
    params = dict(rel_bias=rel_bias, norm1_g=norm1_g, w_in=w_in, sgu_ln_g=sgu_ln_g, sgu_ln_b=sgu_ln_b,
                  sgu_w=sgu_w, sgu_b=sgu_b, qn_b=qn_b, kn_b=kn_b, qn_c=qn_c, kn_c=kn_c,
                  lam_q1=lam_q1, lam_k1=lam_k1, lam_q2=lam_q2, lam_k2=lam_k2, subln_g=subln_g,
                  w_pa=w_pa, w_pb=w_pb, w_pc=w_pc, w_o=w_o, norm2_g=norm2_g, w_gu=w_gu, w_down=w_down)
    return _trunk(x_prompt, x_sample, params)
```

```python
import functools
import math

import numpy as np
import jax
import jax.numpy as jnp
from jax import lax
from jax.experimental import pallas as pl
from jax.experimental.pallas import tpu as pltpu

D_MODEL = 1024
HEAD_DIM = 64
LANES = 128
CHUNK = 128
A_GROUPS = 8
W_A = A_GROUPS * HEAD_DIM
DILATED_GROUPS = ((128, 1), (512, 4), (2048, 16))
HEADS_PER_GROUP = 2
N_GROUPS = len(DILATED_GROUPS)
H_B = HEADS_PER_GROUP * N_GROUPS
W_B = H_B * HEAD_DIM
H_C = 4
W_C_QK = 2 * H_C * HEAD_DIM
W_C = H_C * 2 * HEAD_DIM
NUM_BUCKETS = 32
REL_MAX_DIST = 1024
D_FF = ((8 * D_MODEL + 3 * 256 - 1) // (3 * 256)) * 256
N_BRANCH = 3
RMS_EPS = 1e-6
LN_EPS = 1e-5
NEG_INF = -1e30
BAND = 64
assert all(w // (2 * d) == BAND for w, d in DILATED_GROUPS)
assert HEADS_PER_GROUP * HEAD_DIM == LANES

OFF_A = 0
OFF_B = 2 * W_A
OFF_C = OFF_B + 3 * W_B
OFF_G = OFF_C + 2 * W_C_QK + W_C
IN_COLS = OFF_G + N_BRANCH * D_MODEL
GRP_COLS = 3 * LANES

FF_CHUNK = 256
KEY_CHUNK = 256
ONES_ROWS = 16
LOG2E = math.log2(math.e)
VMEM_LIMIT = 56 * 2 ** 20

_F32 = jnp.float32
_BF16 = jnp.bfloat16


def _in_proj_permutation():
    cols = list(range(OFF_B))
    for g in range(N_GROUPS):
        for part in range(3):
            base = OFF_B + part * W_B + g * LANES
            cols += list(range(base, base + LANES))
    for part in range(2):
        for h in range(H_C):
            for c in range(2):
                base = OFF_C + part * W_C_QK + c * (H_C * HEAD_DIM) + h * HEAD_DIM
                cols += list(range(base, base + HEAD_DIM))
    cols += list(range(OFF_C + 2 * W_C_QK, IN_COLS))
    assert sorted(cols) == list(range(IN_COLS))
    return np.asarray(cols, np.int32)


def _bucket_thresholds(max_dist):
    nb = NUM_BUCKETS // 2
    max_exact = nb // 2
    n = np.arange(1, max_dist + 1)

    def idx(dtype):
        nf = n.astype(dtype)
        large = max_exact + (np.log(nf / dtype(max_exact)) / dtype(math.log(REL_MAX_DIST / max_exact))
                             * dtype(nb - max_exact)).astype(np.int32)
        return np.where(n < max_exact, n, np.minimum(large, nb - 1))

    i32, i64 = idx(np.float32), idx(np.float64)
    assert np.array_equal(i32, i64) and np.all(np.diff(i32) >= 0) and np.all(np.diff(i32) <= 1)
    thr = [0] + [int(n[np.argmax(i32 >= b)]) for b in range(1, nb)]
    return thr


_THR = _bucket_thresholds(1 << 16)


def _bias_from_rel(rel, tab_ref, head):
    nb = NUM_BUCKETS // 2
    val = jnp.full(rel.shape, tab_ref[nb - 1, head], _F32)
    for b in range(nb - 2, -1, -1):
        val = jnp.where(rel >= -(_THR[b + 1] - 1), tab_ref[b, head], val)
    for b in range(1, nb):
        val = jnp.where(rel >= _THR[b], tab_ref[nb + b, head], val)
    return val


def _bias_c_kernel(tab_ref, out_ref, *, tr):
    h = pl.program_id(0)
    i = pl.program_id(1)
    s = out_ref.shape[-1]
    kpos = lax.broadcasted_iota(jnp.int32, (tr, s), 0) + i * tr
    qpos = lax.broadcasted_iota(jnp.int32, (tr, s), 1)
    out_ref[0] = _bias_from_rel(kpos - qpos, tab_ref, H_B + h) * LOG2E


def _bias_c(rel_bias, s):
    tr = min(128, s)
    return pl.pallas_call(
        functools.partial(_bias_c_kernel, tr=tr),
        grid=(H_C, s // tr),
        in_specs=[pl.BlockSpec(memory_space=pltpu.SMEM)],
        out_specs=pl.BlockSpec((1, tr, s), lambda h, i: (h, i, 0)),
        out_shape=jax.ShapeDtypeStruct((H_C, s, s), _F32),
        name="bias_c",
    )(rel_bias)


def _band_tiling(length):
    tq = min(256, length)
    kw = min(tq + 2 * BAND, length)
    nt = length // tq
    assert nt * tq == length
    starts = [min(max(t * tq - BAND, 0), length - kw) for t in range(nt)]
    offs = sorted({starts[t] - t * tq for t in range(nt)})
    variants = [offs.index(starts[t] - t * tq) for t in range(nt)]
    return tq, kw, starts, offs, variants


def _bias_b_kernel(tab_ref, out_ref, *, group, dil, offs):
    _, _, tq, kw = out_ref.shape
    i = lax.broadcasted_iota(jnp.int32, (tq, kw), 0)
    j = lax.broadcasted_iota(jnp.int32, (tq, kw), 1)
    for v, off in enumerate(offs):
        dist = j + off - i
        inside = jnp.abs(dist) <= BAND
        for hh in range(HEADS_PER_GROUP):
            val = _bias_from_rel(dist * dil, tab_ref, group * HEADS_PER_GROUP + hh)
            out_ref[hh, v] = jnp.where(inside, val, NEG_INF)


def _bias_b(rel_bias, group, dil, length):
    tq, kw, _, offs, _ = _band_tiling(length)
    return pl.pallas_call(
        functools.partial(_bias_b_kernel, group=group, dil=dil, offs=offs),
        in_specs=[pl.BlockSpec(memory_space=pltpu.SMEM)],
        out_shape=jax.ShapeDtypeStruct((HEADS_PER_GROUP, len(offs), tq, kw), _F32),
        name=f"bias_b{group}",
    )(rel_bias)


def _head_rms(z, gain):
    lane = lax.broadcasted_iota(jnp.int32, (1, LANES), 1)
    first = lane < HEAD_DIM
    sq = z * z
    s0 = jnp.sum(jnp.where(first, sq, 0.0), axis=-1, keepdims=True)
    s1 = jnp.sum(jnp.where(first, 0.0, sq), axis=-1, keepdims=True)
    ms = jnp.where(first, s0, s1) * (1.0 / HEAD_DIM)
    return (z * lax.rsqrt(ms + RMS_EPS)) * gain


def _sigmoid(x):
    return 1.0 / (1.0 + jnp.exp(-x))


def _gelu_tanh(x):
    c = math.sqrt(2.0 / math.pi)
    return x * (0.5 * (1.0 + jnp.tanh(c * (x + 0.044715 * (x * x * x)))))


def _in_proj_kernel(l_ref, x_ref, n1g_ref, win_ref, lng_ref, lnb_ref, sguw_ref, sgub_ref,
                    qgb_ref, kgb_ref, qgc_ref, kgc_ref, wpa_ref,
                    ma_ref, g12_ref, zb0_ref, zb1_ref, zb2_ref, qc_ref, kc_ref, vct_ref, a_scr):
    del l_ref
    tm = x_ref.shape[0]
    x = x_ref[...]
    h = ((x * lax.rsqrt(jnp.mean(x * x, axis=-1, keepdims=True) + RMS_EPS)) * n1g_ref[0]).astype(_BF16)

    def proj(lo, hi):
        return jnp.dot(h, win_ref[0, :, lo:hi], preferred_element_type=_F32)

    ga = _gelu_tanh(proj(OFF_A, OFF_B))
    u, v = ga[:, :W_A], ga[:, W_A:]
    vc = v - jnp.mean(v, axis=-1, keepdims=True)
    var = jnp.mean(vc * vc, axis=-1, keepdims=True)
    vn = ((vc * lax.rsqrt(var + LN_EPS)) * lng_ref[0] + lnb_ref[0]).astype(_BF16)
    first = lax.broadcasted_iota(jnp.int32, (1, LANES), 1) < HEAD_DIM
    for c in range(tm // CHUNK):
        rows = slice(c * CHUNK, (c + 1) * CHUNK)
        for j in range(A_GROUPS // 2):
            cols = slice(j * LANES, (j + 1) * LANES)
            slab = vn[rows, cols]
            r0 = jnp.dot(sguw_ref[0, 2 * j], slab, preferred_element_type=_F32)
            r1 = jnp.dot(sguw_ref[0, 2 * j + 1], slab, preferred_element_type=_F32)
            sv = jnp.where(first, r0, r1) + sgub_ref[0, :, cols]
            a_scr[rows, cols] = (u[rows, cols] * sv).astype(_BF16)

    gates = _sigmoid(proj(OFF_G, IN_COLS))
    ma_ref[...] = gates[:, :D_MODEL] * jnp.dot(a_scr[...], wpa_ref[0], preferred_element_type=_F32)
    g12_ref[...] = gates[:, D_MODEL:]

    zb = proj(OFF_B, OFF_C)
    for g, out in enumerate((zb0_ref, zb1_ref, zb2_ref)):
        base = g * GRP_COLS
        out[:, 0:LANES] = _head_rms(zb[:, base:base + LANES], qgb_ref[0]) * HEAD_DIM ** -0.5
        out[:, LANES:2 * LANES] = _head_rms(zb[:, base + LANES:base + 2 * LANES], kgb_ref[0])
        out[:, 2 * LANES:] = zb[:, base + 2 * LANES:base + 3 * LANES]

    zc = proj(OFF_C, OFF_G)
    for hh in range(H_C):
        cols = slice(hh * LANES, (hh + 1) * LANES)
        qc_ref[:, cols] = (_head_rms(zc[:, cols], qgc_ref[0]) * (HEAD_DIM ** -0.5 * LOG2E)).astype(_BF16)
        kc_ref[:, cols] = _head_rms(zc[:, W_C_QK + hh * LANES:W_C_QK + (hh + 1) * LANES],
                                    kgc_ref[0]).astype(_BF16)
    vct_ref[...] = zc[:, 2 * W_C_QK:].T.astype(_BF16)


def _layer_spec(shape):
    nd = len(shape)
    return pl.BlockSpec((1,) + tuple(shape[1:]), lambda i, l: (l[0],) + (0,) * (nd - 1),
                        pipeline_mode=pl.Buffered(1))


def _in_proj(l, x, p):
    t = x.shape[0]
    tm = min(512, t)
    row = lambda w: pl.BlockSpec((tm, w), lambda i, l: (i, 0))
    weights = (p["norm1_g"], p["w_in"], p["sgu_ln_g"], p["sgu_ln_b"], p["sgu_w"], p["sgu_b"],
               p["qn_b"], p["kn_b"], p["qn_c"], p["kn_c"], p["w_pa"])
    out_widths = (D_MODEL, 2 * D_MODEL, GRP_COLS, GRP_COLS, GRP_COLS, W_C_QK, W_C_QK)
    out_dtypes = (_F32,) * 5 + (_BF16,) * 2
    return pl.pallas_call(
        _in_proj_kernel,
        grid_spec=pltpu.PrefetchScalarGridSpec(
            num_scalar_prefetch=1,
            grid=(t // tm,),
            in_specs=[row(D_MODEL)] + [_layer_spec(w.shape) for w in weights],
            out_specs=[row(w) for w in out_widths] + [pl.BlockSpec((W_C, tm), lambda i, l: (0, i))],
            scratch_shapes=[pltpu.VMEM((tm, W_A), _BF16)],
        ),
        out_shape=[jax.ShapeDtypeStruct((t, w), dt) for w, dt in zip(out_widths, out_dtypes)]
        + [jax.ShapeDtypeStruct((W_C, t), _BF16)],
        compiler_params=pltpu.CompilerParams(dimension_semantics=("parallel",),
                                             vmem_limit_bytes=VMEM_LIMIT),
        name="in_proj",
    )(l, x, *weights)


def _dilated_kernel(q_ref, k_ref, v_ref, bias_ref, o_ref, lse_ref, *, dil, tq, kw, starts, variants):
    lane = lax.broadcasted_iota(jnp.int32, (1, LANES), 1)
    first = lane < HEAD_DIM
    head_mask = (first.astype(_BF16), (~first).astype(_BF16))
    for r in range(dil):
        for t, (ks, var) in enumerate(zip(starts, variants)):
            q_rows = pl.ds(r + t * tq * dil, tq, stride=dil)
            k_rows = pl.ds(r + ks * dil, kw, stride=dil)
            q = q_ref[0, q_rows, :].astype(_BF16)
            k = k_ref[0, k_rows, :].astype(_BF16)
            v = v_ref[0, k_rows, :].astype(_BF16)
            outs, lses = [], []
            for hh in range(HEADS_PER_GROUP):
                s = lax.dot_general(q * head_mask[hh], k, (((1,), (1,)), ((), ())),
                                    preferred_element_type=_F32) + bias_ref[hh, var]
                m = jnp.max(s, axis=-1, keepdims=True)
                e = jnp.exp(s - m)
                den = jnp.sum(e, axis=-1, keepdims=True)
                outs.append(jnp.dot(e.astype(_BF16), v, preferred_element_type=_F32) * (1.0 / den))
                lses.append(m + jnp.log(den))
            o_ref[0, q_rows, :] = jnp.where(first, outs[0], outs[1])
            lse_ref[0, q_rows, :] = jnp.where(first, lses[0], lses[1])


def _dilated_attention(zb, bias, dil, batch, seq):
    tq, kw, starts, _, variants = _band_tiling(seq // dil)
    zb = zb.reshape(batch, seq, GRP_COLS)
    out = jax.ShapeDtypeStruct((batch, seq, LANES), _F32)
    o, lse = pl.pallas_call(
        functools.partial(_dilated_kernel, dil=dil, tq=tq, kw=kw, starts=starts, variants=variants),
        grid=(batch,),
        in_specs=[pl.BlockSpec((1, seq, LANES), lambda b: (b, 0, 0)),
                  pl.BlockSpec((1, seq, LANES), lambda b: (b, 0, 1)),
                  pl.BlockSpec((1, seq, LANES), lambda b: (b, 0, 2)),
                  pl.BlockSpec(bias.shape, lambda b: (0, 0, 0, 0))],
        out_specs=[pl.BlockSpec((1, seq, LANES), lambda b: (b, 0, 0))] * 2,
        out_shape=[out, out],
        compiler_params=pltpu.CompilerParams(dimension_semantics=("parallel",),
                                             vmem_limit_bytes=VMEM_LIMIT),
        name=f"dilated_d{dil}",
    )(zb, zb, zb, bias)
    return o.reshape(batch * seq, LANES), lse.reshape(batch * seq, LANES)


def _sublane_group_max(s):
    parts = [s[i:i + 8] for i in range(0, s.shape[0], 8)]
    while len(parts) > 1:
        parts = [jnp.maximum(a, b) for a, b in zip(parts[0::2], parts[1::2])] + parts[len(parts) & ~1:]
    return parts[0]


def _diff_kernel(l_ref, lam_init_ref, q_ref, k_ref, vt_ref, bias_ref, lq1_ref, lk1_ref, lq2_ref, lk2_ref,
                 sg_ref, o_ref, s0_scr, s1_scr):
    seq = k_ref.shape[1]
    kc = min(KEY_CHUNK, seq)
    lam_init = lam_init_ref[l_ref[0]]
    lam = (jnp.exp(jnp.sum(lq1_ref[0] * lk1_ref[0], axis=-1, keepdims=True))
           - jnp.exp(jnp.sum(lq2_ref[0] * lk2_ref[0], axis=-1, keepdims=True)) + lam_init)
    first = lax.broadcasted_iota(jnp.int32, (1, LANES), 1) < HEAD_DIM
    q = q_ref[0]
    ones = jnp.ones((ONES_ROWS, kc), _BF16)
    chunks = [slice(j, j + kc) for j in range(0, seq, kc)]

    qm = [q * jnp.where(mask, 1.0, 0.0).astype(_BF16) for mask in (first, ~first)]
    s_scr = (s0_scr, s1_scr)

    def logits_chunk(c, rows, m8):
        s = lax.dot_general(k_ref[0, rows, :], qm[c], (((1,), (1,)), ((), ())),
                            preferred_element_type=_F32) + bias_ref[0, rows, :]
        s_scr[c][rows, :] = s
        cm = _sublane_group_max(s)
        return cm if m8 is None else jnp.maximum(m8, cm)

    def value_chunk(c, rows, m, acc):
        p = jnp.exp2(s_scr[c][rows, :] - m).astype(_BF16)
        d = jnp.dot(jnp.concatenate([vt_ref[:, rows], ones], axis=0), p, preferred_element_type=_F32)
        return d if acc is None else acc + d

    def finish(accs):
        acc = accs[0] if accs[1] is None else accs[0] + accs[1]
        return acc[:LANES] * (1.0 / acc[LANES:LANES + 1])

    maxes = []
    for c in range(2):
        m8 = None
        for rows in chunks:
            m8 = logits_chunk(c, rows, m8)
        maxes.append(jnp.max(m8, axis=0, keepdims=True))
    outs = []
    for c in range(2):
        accs = [None, None]
        for j, rows in enumerate(chunks):
            accs[j % 2] = value_chunk(c, rows, maxes[c], accs[j % 2])
        outs.append(finish(accs))
    o_t = outs[0] - lam * outs[1]
    y_t = o_t * lax.rsqrt(jnp.mean(o_t * o_t, axis=0, keepdims=True) + RMS_EPS)
    o_ref[0] = ((y_t.T * sg_ref[0]) * (1.0 - lam_init)).astype(o_ref.dtype)


def _diff_attention(l, lam_init, qc, kc, vct, bias_t, p, batch, seq):
    tq = min(512, seq)
    qc, kc = (a.reshape(batch, seq, a.shape[-1]) for a in (qc, kc))
    q_spec = pl.BlockSpec((1, tq, LANES), lambda h, i, b, l: (b, i, h))
    vec = lambda w: pl.BlockSpec((1, 1, w), lambda h, i, b, l: (l[0], 0, 0))
    out = pl.pallas_call(
        _diff_kernel,
        grid_spec=pltpu.PrefetchScalarGridSpec(
            num_scalar_prefetch=1,
            grid=(H_C, seq // tq, batch),
            in_specs=[pl.BlockSpec(memory_space=pltpu.SMEM), q_spec,
                      pl.BlockSpec((1, seq, LANES), lambda h, i, b, l: (b, 0, h)),
                      pl.BlockSpec((LANES, seq), lambda h, i, b, l: (h, b)),
                      pl.BlockSpec((1, seq, tq), lambda h, i, b, l: (h, 0, i)),
                      vec(HEAD_DIM), vec(HEAD_DIM), vec(HEAD_DIM), vec(HEAD_DIM), vec(LANES)],
            out_specs=q_spec,
            scratch_shapes=[pltpu.VMEM((seq, tq), _F32)] * 2,
        ),
        out_shape=jax.ShapeDtypeStruct((batch, seq, W_C), _BF16),
        compiler_params=pltpu.CompilerParams(dimension_semantics=("parallel", "parallel", "parallel"),
                                             vmem_limit_bytes=VMEM_LIMIT),
        name="diff_attn",
    )(l, lam_init, qc, kc, vct, bias_t, p["lam_q1"], p["lam_k1"], p["lam_q2"], p["lam_k2"], p["subln_g"])
    return out.reshape(batch * seq, W_C)


def _merge_ffn_kernel(l_ref, x_ref, ma_ref, g12_ref, o0_ref, o1_ref, o2_ref, s0_ref, s1_ref, s2_ref, c_ref,
                      wpb_ref, wpc_ref, wo_ref, n2g_ref, wgu_ref, wdn_ref, out_ref):
    del l_ref
    lses = (s0_ref[...], s1_ref[...], s2_ref[...])
    m = jnp.maximum(jnp.maximum(lses[0], lses[1]), lses[2])
    es = [jnp.exp(s - m) for s in lses]
    inv = 1.0 / (es[0] + es[1] + es[2])
    b_out = jnp.concatenate([o[...] * (e * inv) for o, e in zip((o0_ref, o1_ref, o2_ref), es)], axis=-1)
    pb = jnp.dot(b_out.astype(_BF16), wpb_ref[0], preferred_element_type=_F32)
    pc = jnp.dot(c_ref[...], wpc_ref[0], preferred_element_type=_F32)
    g12 = g12_ref[...]
    merged = ma_ref[...] + g12[:, :D_MODEL] * pb + g12[:, D_MODEL:] * pc
    x1 = x_ref[...] + jnp.dot(merged.astype(_BF16), wo_ref[0], preferred_element_type=_F32)

    h2 = ((x1 * lax.rsqrt(jnp.mean(x1 * x1, axis=-1, keepdims=True) + RMS_EPS)) * n2g_ref[0]).astype(_BF16)
    acc = x1
    for j in range(D_FF // FF_CHUNK):
        lo = j * FF_CHUNK
        g = jnp.dot(h2, wgu_ref[0, :, lo:lo + FF_CHUNK], preferred_element_type=_F32)
        u = jnp.dot(h2, wgu_ref[0, :, D_FF + lo:D_FF + lo + FF_CHUNK], preferred_element_type=_F32)
        act = ((g * _sigmoid(g)) * u).astype(_BF16)
        acc = acc + jnp.dot(act, wdn_ref[0, lo:lo + FF_CHUNK, :], preferred_element_type=_F32)
    out_ref[...] = acc


def _merge_ffn(l, x, ma, g12, o_groups, lse_groups, c_out, p):
    t = x.shape[0]
    tm = min(512, t)
    row = lambda w: pl.BlockSpec((tm, w), lambda i, l: (i, 0))
    weights = (p["w_pb"], p["w_pc"], p["w_o"], p["norm2_g"], p["w_gu"], p["w_down"])
    acts = (x, ma, g12) + tuple(o_groups) + tuple(lse_groups) + (c_out,)
    return pl.pallas_call(
        _merge_ffn_kernel,
        input_output_aliases={1: 0},
        grid_spec=pltpu.PrefetchScalarGridSpec(
            num_scalar_prefetch=1,
            grid=(t // tm,),
            in_specs=[row(a.shape[1]) for a in acts] + [_layer_spec(w.shape) for w in weights],
            out_specs=row(D_MODEL),
        ),
        out_shape=jax.ShapeDtypeStruct((t, D_MODEL), _F32),
        compiler_params=pltpu.CompilerParams(dimension_semantics=("parallel",),
                                             vmem_limit_bytes=VMEM_LIMIT),
        name="merge_ffn",
    )(l, *acts, *weights)


def _layer(l, x, p, biases, lam_init, batch, seq):
    ma, g12, zb0, zb1, zb2, qc, kc, vct = _in_proj(l, x, p)
    o_groups, lse_groups = [], []
    for g, (zb, (_, dil)) in enumerate(zip((zb0, zb1, zb2), DILATED_GROUPS)):
        o, lse = _dilated_attention(zb, biases["b"][g], dil, batch, seq)
        o_groups.append(o)
        lse_groups.append(lse)
    c_out = _diff_attention(l, lam_init, qc, kc, vct, biases["c"], p, batch, seq)
    return _merge_ffn(l, x, ma, g12, o_groups, lse_groups, c_out, p)


def _prepare_params(p):
    depth = p["w_in"].shape[0]
    row = lambda a: a.reshape(depth, 1, a.shape[-1])
    tile2 = lambda a: row(jnp.tile(a, (1, LANES // HEAD_DIM)))
    sgu_b = jnp.broadcast_to(jnp.swapaxes(p["sgu_b"], 1, 2)[..., None], (depth, CHUNK, A_GROUPS, HEAD_DIM))
    return dict(
        norm1_g=row(p["norm1_g"]), norm2_g=row(p["norm2_g"]),
        w_in=jnp.take(p["w_in"], _in_proj_permutation(), axis=2).astype(_BF16),
        sgu_ln_g=row(p["sgu_ln_g"]), sgu_ln_b=row(p["sgu_ln_b"]),
        sgu_w=p["sgu_w"].astype(_BF16), sgu_b=sgu_b.reshape(depth, CHUNK, W_A),
        qn_b=tile2(p["qn_b"]), kn_b=tile2(p["kn_b"]), qn_c=tile2(p["qn_c"]), kn_c=tile2(p["kn_c"]),
        lam_q1=row(p["lam_q1"]), lam_k1=row(p["lam_k1"]), lam_q2=row(p["lam_q2"]), lam_k2=row(p["lam_k2"]),
        subln_g=row(p["subln_g"]),
        w_pa=p["w_pa"].astype(_BF16), w_pb=p["w_pb"].astype(_BF16), w_pc=p["w_pc"].astype(_BF16),
        w_o=p["w_o"].astype(_BF16), w_gu=p["w_gu"].astype(_BF16), w_down=p["w_down"].astype(_BF16),
    )


def _biases(rel_bias, seq):
    return dict(c=_bias_c(rel_bias, seq),
                b=[_bias_b(rel_bias, g, dil, seq // dil) for g, (_, dil) in enumerate(DILATED_GROUPS)])


@jax.jit
def _trunk(x_prompt, x_sample, params):
    p = _prepare_params(params)
    depth = params["w_in"].shape[0]
    lam_init = jnp.asarray([0.8 - 0.6 * math.exp(-0.3 * l) for l in range(depth)], _F32)
    groups = [(x.shape[0], x.shape[1]) for x in (x_prompt, x_sample)]
    biases = [_biases(params["rel_bias"], seq) for _, seq in groups]

    def body(l, xs):
        lidx = jnp.reshape(l, (1,)).astype(jnp.int32)
        return tuple(_layer(lidx, x, p, b, lam_init, batch, seq)
                     for x, b, (batch, seq) in zip(xs, biases, groups))

    xs = tuple(x.reshape(-1, D_MODEL) for x in (x_prompt, x_sample))
    ys = lax.fori_loop(0, depth, body, xs)
    return tuple(y.reshape(x.shape) for y, x in zip(ys, (x_prompt, x_sample)))


def kernel(x_prompt, x_sample, rel_bias, norm1_g, w_in, sgu_ln_g, sgu_ln_b, sgu_w, sgu_b, qn_b, kn_b, qn_c, kn_c, lam_q1, lam_k1, lam_q2, lam_k2, subln_g, w_pa, w_pb, w_pc, w_o, norm2_g, w_gu, w_down):
    params = dict(rel_bias=rel_bias, norm1_g=norm1_g, w_in=w_in, sgu_ln_g=sgu_ln_g, sgu_ln_b=sgu_ln_b,
                  sgu_w=sgu_w, sgu_b=sgu_b, qn_b=qn_b, kn_b=kn_b, qn_c=qn_c, kn_c=kn_c,
                  lam_q1=lam_q1, lam_k1=lam_k1, lam_q2=lam_q2, lam_k2=lam_k2, subln_g=subln_g,
                  w_pa=w_pa, w_pb=w_pb, w_pc=w_pc, w_o=w_o, norm2_g=norm2_g, w_gu=w_gu, w_down=w_down)
    return _trunk(x_prompt, x_sample, params)
```

```python
import functools
import math

import numpy as np
import jax
import jax.numpy as jnp
from jax import lax
from jax.experimental import pallas as pl
from jax.experimental.pallas import tpu as pltpu

D_MODEL = 1024
HEAD_DIM = 64
LANES = 128
CHUNK = 128
A_GROUPS = 8
W_A = A_GROUPS * HEAD_DIM
DILATED_GROUPS = ((128, 1), (512, 4), (2048, 16))
HEADS_PER_GROUP = 2
N_GROUPS = len(DILATED_GROUPS)
H_B = HEADS_PER_GROUP * N_GROUPS
W_B = H_B * HEAD_DIM
H_C = 4
W_C_QK = 2 * H_C * HEAD_DIM
W_C = H_C * 2 * HEAD_DIM
NUM_BUCKETS = 32
REL_MAX_DIST = 1024
D_FF = ((8 * D_MODEL + 3 * 256 - 1) // (3 * 256)) * 256
N_BRANCH = 3
RMS_EPS = 1e-6
LN_EPS = 1e-5
NEG_INF = -1e30
BAND = 64
assert all(w // (2 * d) == BAND for w, d in DILATED_GROUPS)
assert HEADS_PER_GROUP * HEAD_DIM == LANES

OFF_A = 0
OFF_B = 2 * W_A
OFF_C = OFF_B + 3 * W_B
OFF_G = OFF_C + 2 * W_C_QK + W_C
IN_COLS = OFF_G + N_BRANCH * D_MODEL
GRP_COLS = 3 * LANES

FF_CHUNK = 256
DILATED_GROUP_ROWS = 1024
KEY_CHUNK = 256
ONES_ROWS = 16
LOG2E = math.log2(math.e)
VMEM_LIMIT = 56 * 2 ** 20

_F32 = jnp.float32
_BF16 = jnp.bfloat16


def _in_proj_permutation():
    cols = list(range(OFF_B))
    for g in range(N_GROUPS):
        for part in range(3):
            base = OFF_B + part * W_B + g * LANES
            cols += list(range(base, base + LANES))
    for part in range(2):
        for h in range(H_C):
            for c in range(2):
                base = OFF_C + part * W_C_QK + c * (H_C * HEAD_DIM) + h * HEAD_DIM
                cols += list(range(base, base + HEAD_DIM))
    cols += list(range(OFF_C + 2 * W_C_QK, IN_COLS))
    assert sorted(cols) == list(range(IN_COLS))
    return np.asarray(cols, np.int32)


def _bucket_thresholds(max_dist):
    nb = NUM_BUCKETS // 2
    max_exact = nb // 2
    n = np.arange(1, max_dist + 1)

    def idx(dtype):
        nf = n.astype(dtype)
        large = max_exact + (np.log(nf / dtype(max_exact)) / dtype(math.log(REL_MAX_DIST / max_exact))
                             * dtype(nb - max_exact)).astype(np.int32)
        return np.where(n < max_exact, n, np.minimum(large, nb - 1))

    i32, i64 = idx(np.float32), idx(np.float64)
    assert np.array_equal(i32, i64) and np.all(np.diff(i32) >= 0) and np.all(np.diff(i32) <= 1)
    thr = [0] + [int(n[np.argmax(i32 >= b)]) for b in range(1, nb)]
    return thr


_THR = _bucket_thresholds(1 << 16)


def _bias_from_rel(rel, tab_ref, head):
    nb = NUM_BUCKETS // 2
    val = jnp.full(rel.shape, tab_ref[nb - 1, head], _F32)
    for b in range(nb - 2, -1, -1):
        val = jnp.where(rel >= -(_THR[b + 1] - 1), tab_ref[b, head], val)
    for b in range(1, nb):
        val = jnp.where(rel >= _THR[b], tab_ref[nb + b, head], val)
    return val


def _bias_c_kernel(tab_ref, out_ref, band_scr):
    h = pl.program_id(0)
    j = pl.program_id(1)
    _, s, tc = out_ref.shape

    @pl.when(j == 0)
    def _():
        def fill(i, carry):
            y0 = pl.multiple_of(i * tc, tc)
            y = lax.broadcasted_iota(jnp.int32, (tc, tc), 0) + y0
            c = lax.broadcasted_iota(jnp.int32, (tc, tc), 1)
            band_scr[pl.ds(y0, tc), :] = _bias_from_rel(y - (s - tc) - c, tab_ref, H_B + h) * LOG2E
            return carry

        lax.fori_loop(0, band_scr.shape[0] // tc, fill, 0)

    out_ref[0] = band_scr[pl.ds(pl.multiple_of((s - tc) - j * tc, tc), s), :]


def _bias_c(rel_bias, s):
    tc = min(256, s)
    return pl.pallas_call(
        _bias_c_kernel,
        grid=(H_C, s // tc),
        in_specs=[pl.BlockSpec(memory_space=pltpu.SMEM)],
        out_specs=pl.BlockSpec((1, s, tc), lambda h, j: (h, 0, j)),
        out_shape=jax.ShapeDtypeStruct((H_C, s, s), _F32),
        scratch_shapes=[pltpu.VMEM((2 * s - tc, tc), _F32)],
        compiler_params=pltpu.CompilerParams(dimension_semantics=("arbitrary", "arbitrary")),
        name="bias_c",
    )(rel_bias)


def _band_tiling(length):
    tq = min(256, length)
    kw = min(tq + 2 * BAND, length)
    nt = length // tq
    assert nt * tq == length
    starts = [min(max(t * tq - BAND, 0), length - kw) for t in range(nt)]
    offs = sorted({starts[t] - t * tq for t in range(nt)})
    variants = [offs.index(starts[t] - t * tq) for t in range(nt)]
    return tq, kw, starts, offs, variants


def _bias_b_kernel(tab_ref, out_ref, *, group, dil, offs):
    _, _, tq, kw = out_ref.shape
    i = lax.broadcasted_iota(jnp.int32, (tq, kw), 0)
    j = lax.broadcasted_iota(jnp.int32, (tq, kw), 1)
    for v, off in enumerate(offs):
        dist = j + off - i
        inside = jnp.abs(dist) <= BAND
        for hh in range(HEADS_PER_GROUP):
            val = _bias_from_rel(dist * dil, tab_ref, group * HEADS_PER_GROUP + hh)
            out_ref[hh, v] = jnp.where(inside, val, NEG_INF)


def _bias_b(rel_bias, group, dil, length):
    tq, kw, _, offs, _ = _band_tiling(length)
    return pl.pallas_call(
        functools.partial(_bias_b_kernel, group=group, dil=dil, offs=offs),
        in_specs=[pl.BlockSpec(memory_space=pltpu.SMEM)],
        out_shape=jax.ShapeDtypeStruct((HEADS_PER_GROUP, len(offs), tq, kw), _F32),
        name=f"bias_b{group}",
    )(rel_bias)


def _head_rms(z, gain):
    lane = lax.broadcasted_iota(jnp.int32, (1, LANES), 1)
    first = lane < HEAD_DIM
    sq = z * z
    s0 = jnp.sum(jnp.where(first, sq, 0.0), axis=-1, keepdims=True)
    s1 = jnp.sum(jnp.where(first, 0.0, sq), axis=-1, keepdims=True)
    ms = jnp.where(first, s0, s1) * (1.0 / HEAD_DIM)
    return (z * lax.rsqrt(ms + RMS_EPS)) * gain


def _sigmoid(x):
    return 1.0 / (1.0 + jnp.exp(-x))


def _gelu_tanh(x):
    c = math.sqrt(2.0 / math.pi)
    return x * (0.5 * (1.0 + jnp.tanh(c * (x + 0.044715 * (x * x * x)))))


def _in_proj_kernel(l_ref, x_ref, n1g_ref, win_ref, lng_ref, lnb_ref, sguw_ref, sgub_ref,
                    qgb_ref, kgb_ref, qgc_ref, kgc_ref, wpa_ref,
                    ma_ref, g12_ref, zb0_ref, zb1_ref, zb2_ref, qc_ref, kc_ref, vct_ref, a_scr):
    del l_ref
    tm = x_ref.shape[0]
    x = x_ref[...]
    h = ((x * lax.rsqrt(jnp.mean(x * x, axis=-1, keepdims=True) + RMS_EPS)) * n1g_ref[0]).astype(_BF16)

    def proj(lo, hi):
        return jnp.dot(h, win_ref[0, :, lo:hi], preferred_element_type=_F32)

    ga = _gelu_tanh(proj(OFF_A, OFF_B))
    u, v = ga[:, :W_A], ga[:, W_A:]
    vc = v - jnp.mean(v, axis=-1, keepdims=True)
    var = jnp.mean(vc * vc, axis=-1, keepdims=True)
    vn = ((vc * lax.rsqrt(var + LN_EPS)) * lng_ref[0] + lnb_ref[0]).astype(_BF16)
    first = lax.broadcasted_iota(jnp.int32, (1, LANES), 1) < HEAD_DIM
    for c in range(tm // CHUNK):
        rows = slice(c * CHUNK, (c + 1) * CHUNK)
        for j in range(A_GROUPS // 2):
            cols = slice(j * LANES, (j + 1) * LANES)
            slab = vn[rows, cols]
            r0 = jnp.dot(sguw_ref[0, 2 * j], slab, preferred_element_type=_F32)
            r1 = jnp.dot(sguw_ref[0, 2 * j + 1], slab, preferred_element_type=_F32)
            sv = jnp.where(first, r0, r1) + sgub_ref[0, :, cols]
            a_scr[rows, cols] = (u[rows, cols] * sv).astype(_BF16)

    gates = _sigmoid(proj(OFF_G, IN_COLS))
    ma_ref[...] = gates[:, :D_MODEL] * jnp.dot(a_scr[...], wpa_ref[0], preferred_element_type=_F32)
    g12_ref[...] = gates[:, D_MODEL:]

    zb = proj(OFF_B, OFF_C)
    for g, out in enumerate((zb0_ref, zb1_ref, zb2_ref)):
        base = g * GRP_COLS
        out[:, 0:LANES] = _head_rms(zb[:, base:base + LANES], qgb_ref[0]) * HEAD_DIM ** -0.5
        out[:, LANES:2 * LANES] = _head_rms(zb[:, base + LANES:base + 2 * LANES], kgb_ref[0])
        out[:, 2 * LANES:] = zb[:, base + 2 * LANES:base + 3 * LANES]

    zc = proj(OFF_C, OFF_G)
    for hh in range(H_C):
        cols = slice(hh * LANES, (hh + 1) * LANES)
        qc_ref[:, cols] = (_head_rms(zc[:, cols], qgc_ref[0]) * (HEAD_DIM ** -0.5 * LOG2E)).astype(_BF16)
        kc_ref[:, cols] = _head_rms(zc[:, W_C_QK + hh * LANES:W_C_QK + (hh + 1) * LANES],
                                    kgc_ref[0]).astype(_BF16)
    vct_ref[...] = zc[:, 2 * W_C_QK:].T.astype(_BF16)


def _layer_spec(shape):
    nd = len(shape)
    return pl.BlockSpec((1,) + tuple(shape[1:]), lambda i, l: (l[0],) + (0,) * (nd - 1),
                        pipeline_mode=pl.Buffered(1))


def _in_proj(l, x, p):
    t = x.shape[0]
    tm = min(512, t)
    row = lambda w: pl.BlockSpec((tm, w), lambda i, l: (i, 0))
    weights = (p["norm1_g"], p["w_in"], p["sgu_ln_g"], p["sgu_ln_b"], p["sgu_w"], p["sgu_b"],
               p["qn_b"], p["kn_b"], p["qn_c"], p["kn_c"], p["w_pa"])
    out_widths = (D_MODEL, 2 * D_MODEL, GRP_COLS, GRP_COLS, GRP_COLS, W_C_QK, W_C_QK)
    out_dtypes = (_F32,) * 5 + (_BF16,) * 2
    return pl.pallas_call(
        _in_proj_kernel,
        grid_spec=pltpu.PrefetchScalarGridSpec(
            num_scalar_prefetch=1,
            grid=(t // tm,),
            in_specs=[row(D_MODEL)] + [_layer_spec(w.shape) for w in weights],
            out_specs=[row(w) for w in out_widths] + [pl.BlockSpec((W_C, tm), lambda i, l: (0, i))],
            scratch_shapes=[pltpu.VMEM((tm, W_A), _BF16)],
        ),
        out_shape=[jax.ShapeDtypeStruct((t, w), dt) for w, dt in zip(out_widths, out_dtypes)]
        + [jax.ShapeDtypeStruct((W_C, t), _BF16)],
        compiler_params=pltpu.CompilerParams(dimension_semantics=("parallel",),
                                             vmem_limit_bytes=VMEM_LIMIT),
        name="in_proj",
    )(l, x, *weights)


def _dilated_kernel(q_ref, k_ref, v_ref, bias_ref, o_ref, lse_ref, *, dil, tq, kw, starts, variants, group):
    lane = lax.broadcasted_iota(jnp.int32, (1, LANES), 1)
    first = lane < HEAD_DIM
    head_mask = (first.astype(_BF16), (~first).astype(_BF16))
    heads = range(HEADS_PER_GROUP)
    tiles = [(r, t, ks, var) for r in range(dil) for t, (ks, var) in enumerate(zip(starts, variants))]
    for g0 in range(0, len(tiles), group):
        probs = tiles[g0:g0 + group]
        rows, qs, ks_, vs = [], [], [], []
        for r, t, ks, _ in probs:
            q_rows = pl.ds(r + t * tq * dil, tq, stride=dil)
            k_rows = pl.ds(r + ks * dil, kw, stride=dil)
            rows.append(q_rows)
            qs.append(q_ref[0, q_rows, :].astype(_BF16))
            ks_.append(k_ref[0, k_rows, :].astype(_BF16))
            vs.append(v_ref[0, k_rows, :].astype(_BF16))
        s = [[lax.dot_general(q * head_mask[hh], k, (((1,), (1,)), ((), ())),
                              preferred_element_type=_F32) + bias_ref[hh, p[3]] for hh in heads]
             for q, k, p in zip(qs, ks_, probs)]
        m = [[jnp.max(x, axis=-1, keepdims=True) for x in sp] for sp in s]
        e = [[jnp.exp(x - mx) for x, mx in zip(sp, mp)] for sp, mp in zip(s, m)]
        den = [[jnp.sum(x, axis=-1, keepdims=True) for x in ep] for ep in e]
        o = [[jnp.dot(x.astype(_BF16), v, preferred_element_type=_F32) * (1.0 / d) for x, d in zip(ep, dp)]
             for ep, dp, v in zip(e, den, vs)]
        for q_rows, op, mp, dp in zip(rows, o, m, den):
            o_ref[0, q_rows, :] = jnp.where(first, op[0], op[1])
            lse_ref[0, q_rows, :] = jnp.where(first, mp[0] + jnp.log(dp[0]), mp[1] + jnp.log(dp[1]))


def _dilated_attention(zb, bias, dil, batch, seq):
    tq, kw, starts, _, variants = _band_tiling(seq // dil)
    zb = zb.reshape(batch, seq, GRP_COLS)
    out = jax.ShapeDtypeStruct((batch, seq, LANES), _F32)
    o, lse = pl.pallas_call(
        functools.partial(_dilated_kernel, dil=dil, tq=tq, kw=kw, starts=starts, variants=variants,
                          group=max(1, DILATED_GROUP_ROWS // tq)),
        grid=(batch,),
        in_specs=[pl.BlockSpec((1, seq, LANES), lambda b: (b, 0, 0)),
                  pl.BlockSpec((1, seq, LANES), lambda b: (b, 0, 1)),
                  pl.BlockSpec((1, seq, LANES), lambda b: (b, 0, 2)),
                  pl.BlockSpec(bias.shape, lambda b: (0, 0, 0, 0))],
        out_specs=[pl.BlockSpec((1, seq, LANES), lambda b: (b, 0, 0))] * 2,
        out_shape=[out, out],
        compiler_params=pltpu.CompilerParams(dimension_semantics=("parallel",),
                                             vmem_limit_bytes=VMEM_LIMIT),
        name=f"dilated_d{dil}",
    )(zb, zb, zb, bias)
    return o.reshape(batch * seq, LANES), lse.reshape(batch * seq, LANES)


def _sublane_group_max(s):
    parts = [s[i:i + 8] for i in range(0, s.shape[0], 8)]
    while len(parts) > 1:
        parts = [jnp.maximum(a, b) for a, b in zip(parts[0::2], parts[1::2])] + parts[len(parts) & ~1:]
    return parts[0]


def _diff_kernel(l_ref, lam_init_ref, q_ref, k_ref, vt_ref, bias_ref, lq1_ref, lk1_ref, lq2_ref, lk2_ref,
                 sg_ref, o_ref, s0_scr, s1_scr, m_scr):
    seq = k_ref.shape[1]
    tq = q_ref.shape[1]
    kc = min(KEY_CHUNK, seq)
    s_scr = (s0_scr, s1_scr)

    @pl.when(pl.program_id(0) == 0)
    def _():
        s0_scr[...] = jnp.zeros_like(s0_scr)
        s1_scr[...] = jnp.zeros_like(s1_scr)
        m_scr[...] = jnp.zeros_like(m_scr)

    lam_init = lam_init_ref[l_ref[0]]
    lam = (jnp.exp(jnp.sum(lq1_ref[0] * lk1_ref[0], axis=-1, keepdims=True))
           - jnp.exp(jnp.sum(lq2_ref[0] * lk2_ref[0], axis=-1, keepdims=True)) + lam_init)
    first = lax.broadcasted_iota(jnp.int32, (1, LANES), 1) < HEAD_DIM
    q = q_ref[0]
    qm = [q * jnp.where(mask, 1.0, 0.0).astype(_BF16) for mask in (first, ~first)]
    ones = jnp.ones((ONES_ROWS, kc), _BF16)
    chunks = [slice(j, j + kc) for j in range(0, seq, kc)]
    m_prev = [m_scr[c, 0:1, :] for c in range(2)]

    def value_chunk(c, rows, acc):
        p = jnp.exp2(s_scr[c][rows, :] - m_prev[c]).astype(_BF16)
        d = jnp.dot(jnp.concatenate([vt_ref[:, rows], ones], axis=0), p, preferred_element_type=_F32)
        return d if acc is None else acc + d

    def logits_chunk(c, rows, m8):
        s = lax.dot_general(k_ref[0, rows, :], qm[c], (((1,), (1,)), ((), ())),
                            preferred_element_type=_F32) + bias_ref[0, rows, :]
        s_scr[c][rows, :] = s
        cm = _sublane_group_max(s)
        return cm if m8 is None else jnp.maximum(m8, cm)

    outs = []
    for c in range(2):
        m8 = None
        accs = [None, None]
        for j, rows in enumerate(chunks):
            accs[j % 2] = value_chunk(c, rows, accs[j % 2])
            m8 = logits_chunk(c, rows, m8)
        m_scr[c] = jnp.broadcast_to(jnp.max(m8, axis=0, keepdims=True), (8, tq))
        acc = accs[0] if accs[1] is None else accs[0] + accs[1]
        outs.append(acc[:LANES] * (1.0 / acc[LANES:LANES + 1]))
    o_t = outs[0] - lam * outs[1]
    y_t = o_t * lax.rsqrt(jnp.mean(o_t * o_t, axis=0, keepdims=True) + RMS_EPS)
    o_ref[0] = ((y_t.T * sg_ref[0]) * (1.0 - lam_init)).astype(o_ref.dtype)


def _diff_attention(l, lam_init, qc, kc, vct, bias_t, p, batch, seq):
    tq = min(512, seq)
    nq = seq // tq
    n_tiles = H_C * nq * batch
    qc, kc = (a.reshape(batch, seq, a.shape[-1]) for a in (qc, kc))

    def tile(n):
        return n // (nq * batch), (n // batch) % nq, n % batch

    cur = lambda n: tile(jnp.minimum(n, n_tiles - 1))
    prev = lambda n: tile(jnp.maximum(n - 1, 0))

    def q_map(n, l):
        h, i, b = cur(n)
        return b, i, h

    def k_map(n, l):
        h, _, b = cur(n)
        return b, 0, h

    def bias_map(n, l):
        h, i, _ = cur(n)
        return h, 0, i

    def vt_map(n, l):
        h, _, b = prev(n)
        return h, b

    def out_map(n, l):
        h, i, b = prev(n)
        return b, i, h

    vec = lambda w: pl.BlockSpec((1, 1, w), lambda n, l: (l[0], 0, 0))
    out = pl.pallas_call(
        _diff_kernel,
        grid_spec=pltpu.PrefetchScalarGridSpec(
            num_scalar_prefetch=1,
            grid=(n_tiles + 1,),
            in_specs=[pl.BlockSpec(memory_space=pltpu.SMEM),
                      pl.BlockSpec((1, tq, LANES), q_map),
                      pl.BlockSpec((1, seq, LANES), k_map),
                      pl.BlockSpec((LANES, seq), vt_map),
                      pl.BlockSpec((1, seq, tq), bias_map),
                      vec(HEAD_DIM), vec(HEAD_DIM), vec(HEAD_DIM), vec(HEAD_DIM), vec(LANES)],
            out_specs=pl.BlockSpec((1, tq, LANES), out_map),
            scratch_shapes=[pltpu.VMEM((seq, tq), _F32), pltpu.VMEM((seq, tq), _F32),
                            pltpu.VMEM((2, 8, tq), _F32)],
        ),
        out_shape=jax.ShapeDtypeStruct((batch, seq, W_C), _BF16),
        compiler_params=pltpu.CompilerParams(dimension_semantics=("arbitrary",),
                                             vmem_limit_bytes=VMEM_LIMIT),
        name="diff_attn",
    )(l, lam_init, qc, kc, vct, bias_t, p["lam_q1"], p["lam_k1"], p["lam_q2"], p["lam_k2"], p["subln_g"])
    return out.reshape(batch * seq, W_C)


def _merge_ffn_kernel(l_ref, x_ref, ma_ref, g12_ref, o0_ref, o1_ref, o2_ref, s0_ref, s1_ref, s2_ref, c_ref,
                      wpb_ref, wpc_ref, wo_ref, n2g_ref, wgu_ref, wdn_ref, out_ref):
    del l_ref
    lses = (s0_ref[...], s1_ref[...], s2_ref[...])
    m = jnp.maximum(jnp.maximum(lses[0], lses[1]), lses[2])
    es = [jnp.exp(s - m) for s in lses]
    inv = 1.0 / (es[0] + es[1] + es[2])
    b_out = jnp.concatenate([o[...] * (e * inv) for o, e in zip((o0_ref, o1_ref, o2_ref), es)], axis=-1)
    pb = jnp.dot(b_out.astype(_BF16), wpb_ref[0], preferred_element_type=_F32)
    pc = jnp.dot(c_ref[...], wpc_ref[0], preferred_element_type=_F32)
    g12 = g12_ref[...]
    merged = ma_ref[...] + g12[:, :D_MODEL] * pb + g12[:, D_MODEL:] * pc
    x1 = x_ref[...] + jnp.dot(merged.astype(_BF16), wo_ref[0], preferred_element_type=_F32)

    h2 = ((x1 * lax.rsqrt(jnp.mean(x1 * x1, axis=-1, keepdims=True) + RMS_EPS)) * n2g_ref[0]).astype(_BF16)
    acc = x1
    for j in range(D_FF // FF_CHUNK):
        lo = j * FF_CHUNK
        g = jnp.dot(h2, wgu_ref[0, :, lo:lo + FF_CHUNK], preferred_element_type=_F32)
        u = jnp.dot(h2, wgu_ref[0, :, D_FF + lo:D_FF + lo + FF_CHUNK], preferred_element_type=_F32)
        act = ((g * _sigmoid(g)) * u).astype(_BF16)
        acc = acc + jnp.dot(act, wdn_ref[0, lo:lo + FF_CHUNK, :], preferred_element_type=_F32)
    out_ref[...] = acc


def _merge_ffn(l, x, ma, g12, o_groups, lse_groups, c_out, p):
    t = x.shape[0]
    tm = min(512, t)
    row = lambda w: pl.BlockSpec((tm, w), lambda i, l: (i, 0))
    weights = (p["w_pb"], p["w_pc"], p["w_o"], p["norm2_g"], p["w_gu"], p["w_down"])
    acts = (x, ma, g12) + tuple(o_groups) + tuple(lse_groups) + (c_out,)
    return pl.pallas_call(
        _merge_ffn_kernel,
        input_output_aliases={1: 0},
        grid_spec=pltpu.PrefetchScalarGridSpec(
            num_scalar_prefetch=1,
            grid=(t // tm,),
            in_specs=[row(a.shape[1]) for a in acts] + [_layer_spec(w.shape) for w in weights],
            out_specs=row(D_MODEL),
        ),
        out_shape=jax.ShapeDtypeStruct((t, D_MODEL), _F32),
        compiler_params=pltpu.CompilerParams(dimension_semantics=("parallel",),
                                             vmem_limit_bytes=VMEM_LIMIT),
        name="merge_ffn",
    )(l, *acts, *weights)


def _layer(l, x, p, biases, lam_init, batch, seq):
    ma, g12, zb0, zb1, zb2, qc, kc, vct = _in_proj(l, x, p)
    o_groups, lse_groups = [], []
    for g, (zb, (_, dil)) in enumerate(zip((zb0, zb1, zb2), DILATED_GROUPS)):
        o, lse = _dilated_attention(zb, biases["b"][g], dil, batch, seq)
        o_groups.append(o)
        lse_groups.append(lse)
    c_out = _diff_attention(l, lam_init, qc, kc, vct, biases["c"], p, batch, seq)
    return _merge_ffn(l, x, ma, g12, o_groups, lse_groups, c_out, p)


def _prepare_params(p):
    depth = p["w_in"].shape[0]
    row = lambda a: a.reshape(depth, 1, a.shape[-1])
    tile2 = lambda a: row(jnp.tile(a, (1, LANES // HEAD_DIM)))
    sgu_b = jnp.broadcast_to(jnp.swapaxes(p["sgu_b"], 1, 2)[..., None], (depth, CHUNK, A_GROUPS, HEAD_DIM))
    return dict(
        norm1_g=row(p["norm1_g"]), norm2_g=row(p["norm2_g"]),
        w_in=jnp.take(p["w_in"], _in_proj_permutation(), axis=2).astype(_BF16),
        sgu_ln_g=row(p["sgu_ln_g"]), sgu_ln_b=row(p["sgu_ln_b"]),
        sgu_w=p["sgu_w"].astype(_BF16), sgu_b=sgu_b.reshape(depth, CHUNK, W_A),
        qn_b=tile2(p["qn_b"]), kn_b=tile2(p["kn_b"]), qn_c=tile2(p["qn_c"]), kn_c=tile2(p["kn_c"]),
        lam_q1=row(p["lam_q1"]), lam_k1=row(p["lam_k1"]), lam_q2=row(p["lam_q2"]), lam_k2=row(p["lam_k2"]),
        subln_g=row(p["subln_g"]),
        w_pa=p["w_pa"].astype(_BF16), w_pb=p["w_pb"].astype(_BF16), w_pc=p["w_pc"].astype(_BF16),
        w_o=p["w_o"].astype(_BF16), w_gu=p["w_gu"].astype(_BF16), w_down=p["w_down"].astype(_BF16),
    )


def _biases(rel_bias, seq):
    return dict(c=_bias_c(rel_bias, seq),
                b=[_bias_b(rel_bias, g, dil, seq // dil) for g, (_, dil) in enumerate(DILATED_GROUPS)])


@jax.jit
def _trunk(x_prompt, x_sample, params):
    p = _prepare_params(params)
    depth = params["w_in"].shape[0]
    lam_init = jnp.asarray([0.8 - 0.6 * math.exp(-0.3 * l) for l in range(depth)], _F32)
    groups = [(x.shape[0], x.shape[1]) for x in (x_prompt, x_sample)]
    biases = [_biases(params["rel_bias"], seq) for _, seq in groups]

    def body(l, xs):
        lidx = jnp.reshape(l, (1,)).astype(jnp.int32)
        return tuple(_layer(lidx, x, p, b, lam_init, batch, seq)
                     for x, b, (batch, seq) in zip(xs, biases, groups))

    xs = tuple(x.reshape(-1, D_MODEL) for x in (x_prompt, x_sample))
    ys = lax.fori_loop(0, depth, body, xs)
    return tuple(y.reshape(x.shape) for y, x in zip(ys, (x_prompt, x_sample)))


def kernel(x_prompt, x_sample, rel_bias, norm1_g, w_in, sgu_ln_g, sgu_ln_b, sgu_w, sgu_b, qn_b, kn_b, qn_c, kn_c, lam_q1, lam_k1, lam_q2, lam_k2, subln_g, w_pa, w_pb, w_pc, w_o, norm2_g, w_gu, w_down):
    params = dict(rel_bias=rel_bias, norm1_g=norm1_g, w_in=w_in, sgu_ln_g=sgu_ln_g, sgu_ln_b=sgu_ln_b,
                  sgu_w=sgu_w, sgu_b=sgu_b, qn_b=qn_b, kn_b=kn_b, qn_c=qn_c, kn_c=kn_c,
                  lam_q1=lam_q1, lam_k1=lam_k1, lam_q2=lam_q2, lam_k2=lam_k2, subln_g=subln_g,
                  w_pa=w_pa, w_pb=w_pb, w_pc=w_pc, w_o=w_o, norm2_g=norm2_g, w_gu=w_gu, w_down=w_down)
    return _trunk(x_prompt, x_sample, params)
```

```python
import functools
import math

import numpy as np
import jax
import jax.numpy as jnp
from jax import lax
from jax.experimental import pallas as pl
from jax.experimental.pallas import tpu as pltpu

D_MODEL = 1024
HEAD_DIM = 64
LANES = 128
CHUNK = 128
A_GROUPS = 8
W_A = A_GROUPS * HEAD_DIM
DILATED_GROUPS = ((128, 1), (512, 4), (2048, 16))
HEADS_PER_GROUP = 2
N_GROUPS = len(DILATED_GROUPS)
H_B = HEADS_PER_GROUP * N_GROUPS
W_B = H_B * HEAD_DIM
H_C = 4
W_C_QK = 2 * H_C * HEAD_DIM
W_C = H_C * 2 * HEAD_DIM
NUM_BUCKETS = 32
REL_MAX_DIST = 1024
D_FF = ((8 * D_MODEL + 3 * 256 - 1) // (3 * 256)) * 256
N_BRANCH = 3
RMS_EPS = 1e-6
LN_EPS = 1e-5
NEG_INF = -1e30
BAND = 64
assert all(w // (2 * d) == BAND for w, d in DILATED_GROUPS)
assert HEADS_PER_GROUP * HEAD_DIM == LANES

OFF_A = 0
OFF_B = 2 * W_A
OFF_C = OFF_B + 3 * W_B
OFF_G = OFF_C + 2 * W_C_QK + W_C
IN_COLS = OFF_G + N_BRANCH * D_MODEL
GRP_COLS = 3 * LANES

FF_CHUNK = 256
DILATED_GROUP_ROWS = 1024
KEY_CHUNK = 256
DIFF_LOGITS_ELEMS = 2 ** 21
ONES_ROWS = 16
LOG2E = math.log2(math.e)
VMEM_LIMIT = 56 * 2 ** 20

_F32 = jnp.float32
_BF16 = jnp.bfloat16


def _in_proj_permutation():
    cols = list(range(OFF_B))
    for g in range(N_GROUPS):
        for part in range(3):
            base = OFF_B + part * W_B + g * LANES
            cols += list(range(base, base + LANES))
    for part in range(2):
        for h in range(H_C):
            for c in range(2):
                base = OFF_C + part * W_C_QK + c * (H_C * HEAD_DIM) + h * HEAD_DIM
                cols += list(range(base, base + HEAD_DIM))
    cols += list(range(OFF_C + 2 * W_C_QK, IN_COLS))
    assert sorted(cols) == list(range(IN_COLS))
    return np.asarray(cols, np.int32)


def _bucket_thresholds(max_dist):
    nb = NUM_BUCKETS // 2
    max_exact = nb // 2
    n = np.arange(1, max_dist + 1)

    def idx(dtype):
        nf = n.astype(dtype)
        large = max_exact + (np.log(nf / dtype(max_exact)) / dtype(math.log(REL_MAX_DIST / max_exact))
                             * dtype(nb - max_exact)).astype(np.int32)
        return np.where(n < max_exact, n, np.minimum(large, nb - 1))

    i32, i64 = idx(np.float32), idx(np.float64)
    assert np.array_equal(i32, i64) and np.all(np.diff(i32) >= 0) and np.all(np.diff(i32) <= 1)
    thr = [0] + [int(n[np.argmax(i32 >= b)]) for b in range(1, nb)]
    return thr


_THR = _bucket_thresholds(1 << 16)


def _bias_from_rel(rel, tab_ref, head):
    nb = NUM_BUCKETS // 2
    val = jnp.full(rel.shape, tab_ref[nb - 1, head], _F32)
    for b in range(nb - 2, -1, -1):
        val = jnp.where(rel >= -(_THR[b + 1] - 1), tab_ref[b, head], val)
    for b in range(1, nb):
        val = jnp.where(rel >= _THR[b], tab_ref[nb + b, head], val)
    return val


def _bias_c_kernel(tab_ref, out_ref, band_scr):
    h = pl.program_id(0)
    j = pl.program_id(1)
    _, s, tc = out_ref.shape

    @pl.when(j == 0)
    def _():
        def fill(i, carry):
            y0 = pl.multiple_of(i * tc, tc)
            y = lax.broadcasted_iota(jnp.int32, (tc, tc), 0) + y0
            c = lax.broadcasted_iota(jnp.int32, (tc, tc), 1)
            band_scr[pl.ds(y0, tc), :] = _bias_from_rel(y - (s - tc) - c, tab_ref, H_B + h) * LOG2E
            return carry

        lax.fori_loop(0, band_scr.shape[0] // tc, fill, 0)

    out_ref[0] = band_scr[pl.ds(pl.multiple_of((s - tc) - j * tc, tc), s), :]


def _bias_c(rel_bias, s):
    tc = min(256, s)
    return pl.pallas_call(
        _bias_c_kernel,
        grid=(H_C, s // tc),
        in_specs=[pl.BlockSpec(memory_space=pltpu.SMEM)],
        out_specs=pl.BlockSpec((1, s, tc), lambda h, j: (h, 0, j)),
        out_shape=jax.ShapeDtypeStruct((H_C, s, s), _F32),
        scratch_shapes=[pltpu.VMEM((2 * s - tc, tc), _F32)],
        compiler_params=pltpu.CompilerParams(dimension_semantics=("arbitrary", "arbitrary")),
        name="bias_c",
    )(rel_bias)


def _band_tiling(length):
    tq = min(256, length)
    kw = min(tq + 2 * BAND, length)
    nt = length // tq
    assert nt * tq == length
    starts = [min(max(t * tq - BAND, 0), length - kw) for t in range(nt)]
    offs = sorted({starts[t] - t * tq for t in range(nt)})
    variants = [offs.index(starts[t] - t * tq) for t in range(nt)]
    return tq, kw, starts, offs, variants


def _bias_b_kernel(tab_ref, out_ref, *, group, dil, offs):
    _, _, tq, kw = out_ref.shape
    i = lax.broadcasted_iota(jnp.int32, (tq, kw), 0)
    j = lax.broadcasted_iota(jnp.int32, (tq, kw), 1)
    for v, off in enumerate(offs):
        dist = j + off - i
        inside = jnp.abs(dist) <= BAND
        for hh in range(HEADS_PER_GROUP):
            val = _bias_from_rel(dist * dil, tab_ref, group * HEADS_PER_GROUP + hh)
            out_ref[hh, v] = jnp.where(inside, val, NEG_INF)


def _bias_b(rel_bias, group, dil, length):
    tq, kw, _, offs, _ = _band_tiling(length)
    return pl.pallas_call(
        functools.partial(_bias_b_kernel, group=group, dil=dil, offs=offs),
        in_specs=[pl.BlockSpec(memory_space=pltpu.SMEM)],
        out_shape=jax.ShapeDtypeStruct((HEADS_PER_GROUP, len(offs), tq, kw), _F32),
        name=f"bias_b{group}",
    )(rel_bias)


def _head_rms(z, gain):
    lane = lax.broadcasted_iota(jnp.int32, (1, LANES), 1)
    first = lane < HEAD_DIM
    sq = z * z
    s0 = jnp.sum(jnp.where(first, sq, 0.0), axis=-1, keepdims=True)
    s1 = jnp.sum(jnp.where(first, 0.0, sq), axis=-1, keepdims=True)
    ms = jnp.where(first, s0, s1) * (1.0 / HEAD_DIM)
    return (z * lax.rsqrt(ms + RMS_EPS)) * gain


def _sigmoid(x):
    return 1.0 / (1.0 + jnp.exp(-x))


def _gelu_tanh(x):
    c = math.sqrt(2.0 / math.pi)
    return x * (0.5 * (1.0 + jnp.tanh(c * (x + 0.044715 * (x * x * x)))))


def _in_proj_kernel(l_ref, x_ref, n1g_ref, win_ref, lng_ref, lnb_ref, sguw_ref, sgub_ref,
                    qgb_ref, kgb_ref, qgc_ref, kgc_ref, wpa_ref,
                    ma_ref, g12_ref, zb_ref, qkc_ref, vct_ref, a_scr):
    del l_ref
    tm = x_ref.shape[0]
    x = x_ref[...]
    h = ((x * lax.rsqrt(jnp.mean(x * x, axis=-1, keepdims=True) + RMS_EPS)) * n1g_ref[0]).astype(_BF16)

    def proj(lo, hi):
        return jnp.dot(h, win_ref[0, :, lo:hi], preferred_element_type=_F32)

    za = proj(OFF_A, OFF_B)
    gates = _sigmoid(proj(OFF_G, IN_COLS))
    g12_ref[...] = gates[:, D_MODEL:]
    zb = proj(OFF_B, OFF_C)
    zc = proj(OFF_C, OFF_G)

    ga = _gelu_tanh(za)
    u, v = ga[:, :W_A], ga[:, W_A:]
    vc = v - jnp.mean(v, axis=-1, keepdims=True)
    var = jnp.mean(vc * vc, axis=-1, keepdims=True)
    vn = ((vc * lax.rsqrt(var + LN_EPS)) * lng_ref[0] + lnb_ref[0]).astype(_BF16)
    first = lax.broadcasted_iota(jnp.int32, (1, LANES), 1) < HEAD_DIM
    for c in range(tm // CHUNK):
        rows = slice(c * CHUNK, (c + 1) * CHUNK)
        for j in range(A_GROUPS // 2):
            cols = slice(j * LANES, (j + 1) * LANES)
            slab = vn[rows, cols]
            r0 = jnp.dot(sguw_ref[0, 2 * j], slab, preferred_element_type=_F32)
            r1 = jnp.dot(sguw_ref[0, 2 * j + 1], slab, preferred_element_type=_F32)
            sv = jnp.where(first, r0, r1) + sgub_ref[0, :, cols]
            a_scr[rows, cols] = (u[rows, cols] * sv).astype(_BF16)

    ma_ref[...] = gates[:, :D_MODEL] * jnp.dot(a_scr[...], wpa_ref[0], preferred_element_type=_F32)

    for g in range(N_GROUPS):
        base = g * GRP_COLS
        zb_ref[:, base:base + LANES] = _head_rms(zb[:, base:base + LANES], qgb_ref[0]) * HEAD_DIM ** -0.5
        zb_ref[:, base + LANES:base + 2 * LANES] = _head_rms(zb[:, base + LANES:base + 2 * LANES], kgb_ref[0])
        zb_ref[:, base + 2 * LANES:base + 3 * LANES] = zb[:, base + 2 * LANES:base + 3 * LANES]

    for hh in range(H_C):
        cols = slice(hh * LANES, (hh + 1) * LANES)
        k_cols = slice(W_C_QK + hh * LANES, W_C_QK + (hh + 1) * LANES)
        qkc_ref[:, cols] = (_head_rms(zc[:, cols], qgc_ref[0]) * (HEAD_DIM ** -0.5 * LOG2E)).astype(_BF16)
        qkc_ref[:, k_cols] = _head_rms(zc[:, k_cols], kgc_ref[0]).astype(_BF16)
    vct_ref[...] = zc[:, 2 * W_C_QK:].T.astype(_BF16)


def _layer_spec(shape):
    nd = len(shape)
    return pl.BlockSpec((1,) + tuple(shape[1:]), lambda i, l: (l[0],) + (0,) * (nd - 1),
                        pipeline_mode=pl.Buffered(1))


def _in_proj(l, x, p):
    t = x.shape[0]
    tm = min(512, t)
    row = lambda w: pl.BlockSpec((tm, w), lambda i, l: (i, 0))
    weights = (p["norm1_g"], p["w_in"], p["sgu_ln_g"], p["sgu_ln_b"], p["sgu_w"], p["sgu_b"],
               p["qn_b"], p["kn_b"], p["qn_c"], p["kn_c"], p["w_pa"])
    out_widths = (D_MODEL, 2 * D_MODEL, N_GROUPS * GRP_COLS, 2 * W_C_QK)
    out_dtypes = (_F32,) * 3 + (_BF16,)
    return pl.pallas_call(
        _in_proj_kernel,
        grid_spec=pltpu.PrefetchScalarGridSpec(
            num_scalar_prefetch=1,
            grid=(t // tm,),
            in_specs=[row(D_MODEL)] + [_layer_spec(w.shape) for w in weights],
            out_specs=[row(w) for w in out_widths] + [pl.BlockSpec((W_C, tm), lambda i, l: (0, i))],
            scratch_shapes=[pltpu.VMEM((tm, W_A), _BF16)],
        ),
        out_shape=[jax.ShapeDtypeStruct((t, w), dt) for w, dt in zip(out_widths, out_dtypes)]
        + [jax.ShapeDtypeStruct((W_C, t), _BF16)],
        compiler_params=pltpu.CompilerParams(dimension_semantics=("parallel",),
                                             vmem_limit_bytes=VMEM_LIMIT),
        name="in_proj",
    )(l, x, *weights)


def _dilated_kernel(q_ref, k_ref, v_ref, bias_ref, o_ref, lse_ref, *, dil, tq, kw, starts, variants, group):
    lane = lax.broadcasted_iota(jnp.int32, (1, LANES), 1)
    first = lane < HEAD_DIM
    head_mask = (first.astype(_BF16), (~first).astype(_BF16))
    heads = range(HEADS_PER_GROUP)
    tiles = [(r, t, ks, var) for r in range(dil) for t, (ks, var) in enumerate(zip(starts, variants))]
    for g0 in range(0, len(tiles), group):
        probs = tiles[g0:g0 + group]
        rows, qs, ks_, vs = [], [], [], []
        for r, t, ks, _ in probs:
            q_rows = pl.ds(r + t * tq * dil, tq, stride=dil)
            k_rows = pl.ds(r + ks * dil, kw, stride=dil)
            rows.append(q_rows)
            qs.append(q_ref[0, q_rows, :].astype(_BF16))
            ks_.append(k_ref[0, k_rows, :].astype(_BF16))
            vs.append(v_ref[0, k_rows, :].astype(_BF16))
        s = [[lax.dot_general(q * head_mask[hh], k, (((1,), (1,)), ((), ())),
                              preferred_element_type=_F32) + bias_ref[hh, p[3]] for hh in heads]
             for q, k, p in zip(qs, ks_, probs)]
        m = [[jnp.max(x, axis=-1, keepdims=True) for x in sp] for sp in s]
        e = [[jnp.exp(x - mx) for x, mx in zip(sp, mp)] for sp, mp in zip(s, m)]
        den = [[jnp.sum(x, axis=-1, keepdims=True) for x in ep] for ep in e]
        o = [[jnp.dot(x.astype(_BF16), v, preferred_element_type=_F32) * (1.0 / d) for x, d in zip(ep, dp)]
             for ep, dp, v in zip(e, den, vs)]
        for q_rows, op, mp, dp in zip(rows, o, m, den):
            o_ref[0, q_rows, :] = jnp.where(first, op[0], op[1])
            lse_ref[0, q_rows, :] = jnp.where(first, mp[0] + jnp.log(dp[0]), mp[1] + jnp.log(dp[1]))


def _dilated_attention(zb, bias, group, dil, batch, seq):
    tq, kw, starts, _, variants = _band_tiling(seq // dil)
    zb = zb.reshape(batch, seq, N_GROUPS * GRP_COLS)
    out = jax.ShapeDtypeStruct((batch, seq, LANES), _F32)
    o, lse = pl.pallas_call(
        functools.partial(_dilated_kernel, dil=dil, tq=tq, kw=kw, starts=starts, variants=variants,
                          group=max(1, DILATED_GROUP_ROWS // tq)),
        grid=(batch,),
        in_specs=[pl.BlockSpec((1, seq, LANES), lambda b: (b, 0, 3 * group)),
                  pl.BlockSpec((1, seq, LANES), lambda b: (b, 0, 3 * group + 1)),
                  pl.BlockSpec((1, seq, LANES), lambda b: (b, 0, 3 * group + 2)),
                  pl.BlockSpec(bias.shape, lambda b: (0, 0, 0, 0))],
        out_specs=[pl.BlockSpec((1, seq, LANES), lambda b: (b, 0, 0))] * 2,
        out_shape=[out, out],
        compiler_params=pltpu.CompilerParams(dimension_semantics=("parallel",),
                                             vmem_limit_bytes=VMEM_LIMIT),
        name=f"dilated_d{dil}",
    )(zb, zb, zb, bias)
    return o.reshape(batch * seq, LANES), lse.reshape(batch * seq, LANES)


def _sublane_group_max(s):
    parts = [s[i:i + 8] for i in range(0, s.shape[0], 8)]
    while len(parts) > 1:
        parts = [jnp.maximum(a, b) for a, b in zip(parts[0::2], parts[1::2])] + parts[len(parts) & ~1:]
    return parts[0]


def _diff_kernel(l_ref, lam_init_ref, q_ref, k_ref, vt_ref, bias_ref, lq1_ref, lk1_ref, lq2_ref, lk2_ref,
                 sg_ref, o_ref, s0_scr, s1_scr, m_scr):
    seq = k_ref.shape[1]
    tq = q_ref.shape[1]
    kc = min(KEY_CHUNK, seq)
    s_scr = (s0_scr, s1_scr)

    @pl.when(pl.program_id(0) == 0)
    def _():
        s0_scr[...] = jnp.zeros_like(s0_scr)
        s1_scr[...] = jnp.zeros_like(s1_scr)
        m_scr[...] = jnp.zeros_like(m_scr)

    lam_init = lam_init_ref[l_ref[0]]
    lam = (jnp.exp(jnp.sum(lq1_ref[0] * lk1_ref[0], axis=-1, keepdims=True))
           - jnp.exp(jnp.sum(lq2_ref[0] * lk2_ref[0], axis=-1, keepdims=True)) + lam_init)
    first = lax.broadcasted_iota(jnp.int32, (1, LANES), 1) < HEAD_DIM
    q = q_ref[0]
    qm = [q * jnp.where(mask, 1.0, 0.0).astype(_BF16) for mask in (first, ~first)]
    ones = jnp.ones((ONES_ROWS, kc), _BF16)
    chunks = [slice(j, j + kc) for j in range(0, seq, kc)]
    m_prev = [m_scr[c, 0:1, :] for c in range(2)]

    def value_chunk(c, rows, acc):
        p = jnp.exp2(s_scr[c][rows, :] - m_prev[c]).astype(_BF16)
        d = jnp.dot(jnp.concatenate([vt_ref[:, rows], ones], axis=0), p, preferred_element_type=_F32)
        return d if acc is None else acc + d

    def logits_chunk(c, rows, m8):
        s = lax.dot_general(k_ref[0, rows, :], qm[c], (((1,), (1,)), ((), ())),
                            preferred_element_type=_F32) + bias_ref[0, rows, :]
        s_scr[c][rows, :] = s
        cm = _sublane_group_max(s)
        return cm if m8 is None else jnp.maximum(m8, cm)

    outs = []
    for c in range(2):
        m8 = None
        accs = [None, None]
        for j, rows in enumerate(chunks):
            accs[j % 2] = value_chunk(c, rows, accs[j % 2])
            m8 = logits_chunk(c, rows, m8)
        m_scr[c] = jnp.broadcast_to(jnp.max(m8, axis=0, keepdims=True), (8, tq))
        acc = accs[0] if accs[1] is None else accs[0] + accs[1]
        outs.append(acc[:LANES] * (1.0 / acc[LANES:LANES + 1]))
    o_t = outs[0] - lam * outs[1]
    y_t = o_t * lax.rsqrt(jnp.mean(o_t * o_t, axis=0, keepdims=True) + RMS_EPS)
    o_ref[0] = ((y_t.T * sg_ref[0]) * (1.0 - lam_init)).astype(o_ref.dtype)


def _diff_attention(l, lam_init, qkc, vct, bias_t, p, batch, seq):
    tq = min(seq, DIFF_LOGITS_ELEMS // seq)
    nq = seq // tq
    n_tiles = H_C * nq * batch
    qkc = qkc.reshape(batch, seq, qkc.shape[-1])

    def tile(n):
        return n // (nq * batch), (n // batch) % nq, n % batch

    cur = lambda n: tile(jnp.minimum(n, n_tiles - 1))
    prev = lambda n: tile(jnp.maximum(n - 1, 0))

    def q_map(n, l):
        h, i, b = cur(n)
        return b, i, h

    def k_map(n, l):
        h, _, b = cur(n)
        return b, 0, H_C + h

    def bias_map(n, l):
        h, i, _ = cur(n)
        return h, 0, i

    def vt_map(n, l):
        h, _, b = prev(n)
        return h, b

    def out_map(n, l):
        h, i, b = prev(n)
        return b, i, h

    vec = lambda w: pl.BlockSpec((1, 1, w), lambda n, l: (l[0], 0, 0))
    out = pl.pallas_call(
        _diff_kernel,
        grid_spec=pltpu.PrefetchScalarGridSpec(
            num_scalar_prefetch=1,
            grid=(n_tiles + 1,),
            in_specs=[pl.BlockSpec(memory_space=pltpu.SMEM),
                      pl.BlockSpec((1, tq, LANES), q_map),
                      pl.BlockSpec((1, seq, LANES), k_map),
                      pl.BlockSpec((LANES, seq), vt_map),
                      pl.BlockSpec((1, seq, tq), bias_map),
                      vec(HEAD_DIM), vec(HEAD_DIM), vec(HEAD_DIM), vec(HEAD_DIM), vec(LANES)],
            out_specs=pl.BlockSpec((1, tq, LANES), out_map),
            scratch_shapes=[pltpu.VMEM((seq, tq), _F32), pltpu.VMEM((seq, tq), _F32),
                            pltpu.VMEM((2, 8, tq), _F32)],
        ),
        out_shape=jax.ShapeDtypeStruct((batch, seq, W_C), _BF16),
        compiler_params=pltpu.CompilerParams(dimension_semantics=("arbitrary",),
                                             vmem_limit_bytes=VMEM_LIMIT),
        name="diff_attn",
    )(l, lam_init, qkc, qkc, vct, bias_t, p["lam_q1"], p["lam_k1"], p["lam_q2"], p["lam_k2"], p["subln_g"])
    return out.reshape(batch * seq, W_C)


def _merge_ffn_kernel(l_ref, x_ref, ma_ref, g12_ref, o0_ref, o1_ref, o2_ref, s0_ref, s1_ref, s2_ref, c_ref,
                      wpb_ref, wpc_ref, wo_ref, n2g_ref, wgu_ref, wdn_ref, out_ref):
    del l_ref
    lses = (s0_ref[...], s1_ref[...], s2_ref[...])
    m = jnp.maximum(jnp.maximum(lses[0], lses[1]), lses[2])
    es = [jnp.exp(s - m) for s in lses]
    inv = 1.0 / (es[0] + es[1] + es[2])
    b_out = jnp.concatenate([o[...] * (e * inv) for o, e in zip((o0_ref, o1_ref, o2_ref), es)], axis=-1)
    pb = jnp.dot(b_out.astype(_BF16), wpb_ref[0], preferred_element_type=_F32)
    pc = jnp.dot(c_ref[...], wpc_ref[0], preferred_element_type=_F32)
    g12 = g12_ref[...]
    merged = ma_ref[...] + g12[:, :D_MODEL] * pb + g12[:, D_MODEL:] * pc
    x1 = x_ref[...] + jnp.dot(merged.astype(_BF16), wo_ref[0], preferred_element_type=_F32)

    h2 = ((x1 * lax.rsqrt(jnp.mean(x1 * x1, axis=-1, keepdims=True) + RMS_EPS)) * n2g_ref[0]).astype(_BF16)
    acc = x1
    for j in range(D_FF // FF_CHUNK):
        lo = j * FF_CHUNK
        g = jnp.dot(h2, wgu_ref[0, :, lo:lo + FF_CHUNK], preferred_element_type=_F32)
        u = jnp.dot(h2, wgu_ref[0, :, D_FF + lo:D_FF + lo + FF_CHUNK], preferred_element_type=_F32)
        act = ((g * _sigmoid(g)) * u).astype(_BF16)
        acc = acc + jnp.dot(act, wdn_ref[0, lo:lo + FF_CHUNK, :], preferred_element_type=_F32)
    out_ref[...] = acc


def _merge_ffn(l, x, ma, g12, o_groups, lse_groups, c_out, p, in_place):
    t = x.shape[0]
    tm = min(512, t)
    row = lambda w: pl.BlockSpec((tm, w), lambda i, l: (i, 0))
    weights = (p["w_pb"], p["w_pc"], p["w_o"], p["norm2_g"], p["w_gu"], p["w_down"])
    acts = (x, ma, g12) + tuple(o_groups) + tuple(lse_groups) + (c_out,)
    return pl.pallas_call(
        _merge_ffn_kernel,
        input_output_aliases={1: 0} if in_place else {},
        grid_spec=pltpu.PrefetchScalarGridSpec(
            num_scalar_prefetch=1,
            grid=(t // tm,),
            in_specs=[row(a.shape[1]) for a in acts] + [_layer_spec(w.shape) for w in weights],
            out_specs=row(D_MODEL),
        ),
        out_shape=jax.ShapeDtypeStruct((t, D_MODEL), _F32),
        compiler_params=pltpu.CompilerParams(dimension_semantics=("parallel",),
                                             vmem_limit_bytes=VMEM_LIMIT),
        name="merge_ffn",
    )(l, *acts, *weights)


def _layer(l, x, p, biases, lam_init, batch, seq, in_place):
    ma, g12, zb, qkc, vct = _in_proj(l, x, p)
    o_groups, lse_groups = [], []
    for g, (_, dil) in enumerate(DILATED_GROUPS):
        o, lse = _dilated_attention(zb, biases["b"][g], g, dil, batch, seq)
        o_groups.append(o)
        lse_groups.append(lse)
    c_out = _diff_attention(l, lam_init, qkc, vct, biases["c"], p, batch, seq)
    return _merge_ffn(l, x, ma, g12, o_groups, lse_groups, c_out, p, in_place)


def _prepare_params(p):
    depth = p["w_in"].shape[0]
    row = lambda a: a.reshape(depth, 1, a.shape[-1])
    tile2 = lambda a: row(jnp.tile(a, (1, LANES // HEAD_DIM)))
    sgu_b = jnp.broadcast_to(jnp.swapaxes(p["sgu_b"], 1, 2)[..., None], (depth, CHUNK, A_GROUPS, HEAD_DIM))
    return dict(
        norm1_g=row(p["norm1_g"]), norm2_g=row(p["norm2_g"]),
        w_in=jnp.take(p["w_in"], _in_proj_permutation(), axis=2).astype(_BF16),
        sgu_ln_g=row(p["sgu_ln_g"]), sgu_ln_b=row(p["sgu_ln_b"]),
        sgu_w=p["sgu_w"].astype(_BF16), sgu_b=sgu_b.reshape(depth, CHUNK, W_A),
        qn_b=tile2(p["qn_b"]), kn_b=tile2(p["kn_b"]), qn_c=tile2(p["qn_c"]), kn_c=tile2(p["kn_c"]),
        lam_q1=row(p["lam_q1"]), lam_k1=row(p["lam_k1"]), lam_q2=row(p["lam_q2"]), lam_k2=row(p["lam_k2"]),
        subln_g=row(p["subln_g"]),
        w_pa=p["w_pa"].astype(_BF16), w_pb=p["w_pb"].astype(_BF16), w_pc=p["w_pc"].astype(_BF16),
        w_o=p["w_o"].astype(_BF16), w_gu=p["w_gu"].astype(_BF16), w_down=p["w_down"].astype(_BF16),
    )


def _biases(rel_bias, seq):
    return dict(c=_bias_c(rel_bias, seq),
                b=[_bias_b(rel_bias, g, dil, seq // dil) for g, (_, dil) in enumerate(DILATED_GROUPS)])


@jax.jit
def _trunk(x_prompt, x_sample, params):
    p = _prepare_params(params)
    depth = params["w_in"].shape[0]
    lam_init = jnp.asarray([0.8 - 0.6 * math.exp(-0.3 * l) for l in range(depth)], _F32)
    groups = [(x.shape[0], x.shape[1]) for x in (x_prompt, x_sample)]
    biases = [_biases(params["rel_bias"], seq) for _, seq in groups]

    def layer(l, xs, in_place):
        lidx = jnp.reshape(l, (1,)).astype(jnp.int32)
        return tuple(_layer(lidx, x, p, b, lam_init, batch, seq, in_place)
                     for x, b, (batch, seq) in zip(xs, biases, groups))

    xs = tuple(x.reshape(-1, D_MODEL) for x in (x_prompt, x_sample))
    xs = layer(jnp.int32(0), xs, in_place=False)
    ys = lax.fori_loop(1, depth, functools.partial(layer, in_place=True), xs)
    return tuple(y.reshape(x.shape) for y, x in zip(ys, (x_prompt, x_sample)))


def kernel(x_prompt, x_sample, rel_bias, norm1_g, w_in, sgu_ln_g, sgu_ln_b, sgu_w, sgu_b, qn_b, kn_b, qn_c, kn_c, lam_q1, lam_k1, lam_q2, lam_k2, subln_g, w_pa, w_pb, w_pc, w_o, norm2_g, w_gu, w_down):
    params = dict(rel_bias=rel_bias, norm1_g=norm1_g, w_in=w_in, sgu_ln_g=sgu_ln_g, sgu_ln_b=sgu_ln_b,
                  sgu_w=sgu_w, sgu_b=sgu_b, qn_b=qn_b, kn_b=kn_b, qn_c=qn_c, kn_c=kn_c,
                  lam_q1=lam_q1, lam_k1=lam_k1, lam_q2=lam_q2, lam_k2=lam_k2, subln_g=subln_g,
                  w_pa=w_pa, w_pb=w_pb, w_pc=w_pc, w_o=w_o, norm2_g=norm2_g, w_gu=w_gu, w_down=w_down)
    return _trunk(x_prompt, x_sample, params)
```

```python
import functools
import math

import numpy as np
import jax
import jax.numpy as jnp
from jax import lax
from jax.experimental import pallas as pl
from jax.experimental.pallas import tpu as pltpu

D_MODEL = 1024
HEAD_DIM = 64
LANES = 128
CHUNK = 128
A_GROUPS = 8
W_A = A_GROUPS * HEAD_DIM
DILATED_GROUPS = ((128, 1), (512, 4), (2048, 16))
HEADS_PER_GROUP = 2
N_GROUPS = len(DILATED_GROUPS)
H_B = HEADS_PER_GROUP * N_GROUPS
W_B = H_B * HEAD_DIM
H_C = 4
W_C_QK = 2 * H_C * HEAD_DIM
W_C = H_C * 2 * HEAD_DIM
NUM_BUCKETS = 32
REL_MAX_DIST = 1024
D_FF = ((8 * D_MODEL + 3 * 256 - 1) // (3 * 256)) * 256
N_BRANCH = 3
RMS_EPS = 1e-6
LN_EPS = 1e-5
NEG_INF = -1e30
BAND = 64
assert all(w // (2 * d) == BAND for w, d in DILATED_GROUPS)
assert HEADS_PER_GROUP * HEAD_DIM == LANES

OFF_A = 0
OFF_B = 2 * W_A
OFF_C = OFF_B + 3 * W_B
OFF_G = OFF_C + 2 * W_C_QK + W_C
IN_COLS = OFF_G + N_BRANCH * D_MODEL
GRP_COLS = 3 * LANES

FF_CHUNK = 256
DILATED_GROUP_ROWS = 1024
KEY_CHUNK = 256
DIFF_LOGITS_ELEMS = 2 ** 21
ONES_ROWS = 16
LOG2E = math.log2(math.e)
VMEM_LIMIT = 56 * 2 ** 20

_F32 = jnp.float32
_BF16 = jnp.bfloat16


def _in_proj_permutation():
    cols = list(range(OFF_B))
    for g in range(N_GROUPS):
        for part in range(3):
            base = OFF_B + part * W_B + g * LANES
            cols += list(range(base, base + LANES))
    for part in range(2):
        for h in range(H_C):
            for c in range(2):
                base = OFF_C + part * W_C_QK + c * (H_C * HEAD_DIM) + h * HEAD_DIM
                cols += list(range(base, base + HEAD_DIM))
    cols += list(range(OFF_C + 2 * W_C_QK, IN_COLS))
    assert sorted(cols) == list(range(IN_COLS))
    return np.asarray(cols, np.int32)


def _bucket_thresholds(max_dist):
    nb = NUM_BUCKETS // 2
    max_exact = nb // 2
    n = np.arange(1, max_dist + 1)

    def idx(dtype):
        nf = n.astype(dtype)
        large = max_exact + (np.log(nf / dtype(max_exact)) / dtype(math.log(REL_MAX_DIST / max_exact))
                             * dtype(nb - max_exact)).astype(np.int32)
        return np.where(n < max_exact, n, np.minimum(large, nb - 1))

    i32, i64 = idx(np.float32), idx(np.float64)
    assert np.array_equal(i32, i64) and np.all(np.diff(i32) >= 0) and np.all(np.diff(i32) <= 1)
    thr = [0] + [int(n[np.argmax(i32 >= b)]) for b in range(1, nb)]
    return thr


_THR = _bucket_thresholds(1 << 16)


def _bias_from_rel(rel, tab_ref, head):
    nb = NUM_BUCKETS // 2
    val = jnp.full(rel.shape, tab_ref[nb - 1, head], _F32)
    for b in range(nb - 2, -1, -1):
        val = jnp.where(rel >= -(_THR[b + 1] - 1), tab_ref[b, head], val)
    for b in range(1, nb):
        val = jnp.where(rel >= _THR[b], tab_ref[nb + b, head], val)
    return val


def _bias_c_kernel(tab_ref, out_ref, band_scr):
    h = pl.program_id(0)
    j = pl.program_id(1)
    _, s, tc = out_ref.shape

    @pl.when(j == 0)
    def _():
        def fill(i, carry):
            y0 = pl.multiple_of(i * tc, tc)
            y = lax.broadcasted_iota(jnp.int32, (tc, tc), 0) + y0
            c = lax.broadcasted_iota(jnp.int32, (tc, tc), 1)
            band_scr[pl.ds(y0, tc), :] = _bias_from_rel(y - (s - tc) - c, tab_ref, H_B + h) * LOG2E
            return carry

        lax.fori_loop(0, band_scr.shape[0] // tc, fill, 0)

    out_ref[0] = band_scr[pl.ds(pl.multiple_of((s - tc) - j * tc, tc), s), :]


def _bias_c(rel_bias, s):
    tc = min(256, s)
    return pl.pallas_call(
        _bias_c_kernel,
        grid=(H_C, s // tc),
        in_specs=[pl.BlockSpec(memory_space=pltpu.SMEM)],
        out_specs=pl.BlockSpec((1, s, tc), lambda h, j: (h, 0, j)),
        out_shape=jax.ShapeDtypeStruct((H_C, s, s), _F32),
        scratch_shapes=[pltpu.VMEM((2 * s - tc, tc), _F32)],
        compiler_params=pltpu.CompilerParams(dimension_semantics=("arbitrary", "arbitrary")),
        name="bias_c",
    )(rel_bias)


def _band_tiling(length):
    tq = min(256, length)
    kw = min(tq + 2 * BAND, length)
    nt = length // tq
    assert nt * tq == length
    starts = [min(max(t * tq - BAND, 0), length - kw) for t in range(nt)]
    offs = sorted({starts[t] - t * tq for t in range(nt)})
    variants = [offs.index(starts[t] - t * tq) for t in range(nt)]
    return tq, kw, starts, offs, variants


def _bias_b_kernel(tab_ref, out_ref, *, group, dil, offs):
    _, _, tq, kw = out_ref.shape
    i = lax.broadcasted_iota(jnp.int32, (tq, kw), 0)
    j = lax.broadcasted_iota(jnp.int32, (tq, kw), 1)
    for v, off in enumerate(offs):
        dist = j + off - i
        inside = jnp.abs(dist) <= BAND
        for hh in range(HEADS_PER_GROUP):
            val = _bias_from_rel(dist * dil, tab_ref, group * HEADS_PER_GROUP + hh)
            out_ref[hh, v] = jnp.where(inside, val, NEG_INF)


def _bias_b(rel_bias, group, dil, length):
    tq, kw, _, offs, _ = _band_tiling(length)
    return pl.pallas_call(
        functools.partial(_bias_b_kernel, group=group, dil=dil, offs=offs),
        in_specs=[pl.BlockSpec(memory_space=pltpu.SMEM)],
        out_shape=jax.ShapeDtypeStruct((HEADS_PER_GROUP, len(offs), tq, kw), _F32),
        name=f"bias_b{group}",
    )(rel_bias)


def _head_rms(z, gain):
    lane = lax.broadcasted_iota(jnp.int32, (1, LANES), 1)
    first = lane < HEAD_DIM
    sq = z * z
    s0 = jnp.sum(jnp.where(first, sq, 0.0), axis=-1, keepdims=True)
    s1 = jnp.sum(jnp.where(first, 0.0, sq), axis=-1, keepdims=True)
    ms = jnp.where(first, s0, s1) * (1.0 / HEAD_DIM)
    return (z * lax.rsqrt(ms + RMS_EPS)) * gain


def _sigmoid(x):
    return 1.0 / (1.0 + jnp.exp(-x))


def _gelu_tanh(x):
    c = math.sqrt(2.0 / math.pi)
    return x * (0.5 * (1.0 + jnp.tanh(c * (x + 0.044715 * (x * x * x)))))


def _in_proj_kernel(l_ref, x_ref, n1g_ref, win_ref, lng_ref, lnb_ref, sguw_ref, sgub_ref,
                    qgb_ref, kgb_ref, qgc_ref, kgc_ref, wpa_ref,
                    ma_ref, g12_ref, zb_ref, qkc_ref, vct_ref, a_scr):
    del l_ref
    tm = x_ref.shape[0]
    first = lax.broadcasted_iota(jnp.int32, (1, LANES), 1) < HEAD_DIM

    def project(rows):
        x = x_ref[rows, :]
        h = ((x * lax.rsqrt(jnp.mean(x * x, axis=-1, keepdims=True) + RMS_EPS)) * n1g_ref[0]).astype(_BF16)
        proj = lambda lo, hi: jnp.dot(h, win_ref[0, :, lo:hi], preferred_element_type=_F32)
        za = proj(OFF_A, OFF_B)
        gates = _sigmoid(proj(OFF_G, IN_COLS))
        g12_ref[rows, :] = gates[:, D_MODEL:]
        return za, gates[:, :D_MODEL], proj(OFF_B, OFF_C), proj(OFF_C, OFF_G)

    def mix(rows, za, gate0, zb, zc):
        ga = _gelu_tanh(za)
        u, v = ga[:, :W_A], ga[:, W_A:]
        vc = v - jnp.mean(v, axis=-1, keepdims=True)
        var = jnp.mean(vc * vc, axis=-1, keepdims=True)
        vn = ((vc * lax.rsqrt(var + LN_EPS)) * lng_ref[0] + lnb_ref[0]).astype(_BF16)
        for c in range(za.shape[0] // CHUNK):
            chunk = slice(c * CHUNK, (c + 1) * CHUNK)
            dst = slice(rows.start + c * CHUNK, rows.start + (c + 1) * CHUNK)
            for j in range(A_GROUPS // 2):
                cols = slice(j * LANES, (j + 1) * LANES)
                slab = vn[chunk, cols]
                r0 = jnp.dot(sguw_ref[0, 2 * j], slab, preferred_element_type=_F32)
                r1 = jnp.dot(sguw_ref[0, 2 * j + 1], slab, preferred_element_type=_F32)
                sv = jnp.where(first, r0, r1) + sgub_ref[0, :, cols]
                a_scr[dst, cols] = (u[chunk, cols] * sv).astype(_BF16)
        ma_ref[rows, :] = gate0 * jnp.dot(a_scr[rows, :], wpa_ref[0], preferred_element_type=_F32)

        for g in range(N_GROUPS):
            base = g * GRP_COLS
            zb_ref[3 * g, rows, :] = _head_rms(zb[:, base:base + LANES], qgb_ref[0]) * HEAD_DIM ** -0.5
            zb_ref[3 * g + 1, rows, :] = _head_rms(zb[:, base + LANES:base + 2 * LANES], kgb_ref[0])
            zb_ref[3 * g + 2, rows, :] = zb[:, base + 2 * LANES:base + 3 * LANES]

        for hh in range(H_C):
            cols = slice(hh * LANES, (hh + 1) * LANES)
            k_cols = slice(W_C_QK + hh * LANES, W_C_QK + (hh + 1) * LANES)
            qkc_ref[hh, rows, :] = (_head_rms(zc[:, cols], qgc_ref[0]) * (HEAD_DIM ** -0.5 * LOG2E)).astype(_BF16)
            qkc_ref[H_C + hh, rows, :] = _head_rms(zc[:, k_cols], kgc_ref[0]).astype(_BF16)
        vct_ref[:, rows] = zc[:, 2 * W_C_QK:].T.astype(_BF16)

    n_split = 2 if tm % (2 * LANES) == 0 else 1
    parts = [slice(i * tm // n_split, (i + 1) * tm // n_split) for i in range(n_split)]
    projected = [project(rows) for rows in parts]
    for rows, vals in zip(parts, projected):
        mix(rows, *vals)


def _layer_spec(shape):
    nd = len(shape)
    return pl.BlockSpec((1,) + tuple(shape[1:]), lambda i, l: (l[0],) + (0,) * (nd - 1),
                        pipeline_mode=pl.Buffered(1))


def _in_proj(l, x, p):
    t = x.shape[0]
    tm = min(512, t)
    row = lambda w: pl.BlockSpec((tm, w), lambda i, l: (i, 0))
    weights = (p["norm1_g"], p["w_in"], p["sgu_ln_g"], p["sgu_ln_b"], p["sgu_w"], p["sgu_b"],
               p["qn_b"], p["kn_b"], p["qn_c"], p["kn_c"], p["w_pa"])
    slabs = lambda n: pl.BlockSpec((n, tm, LANES), lambda i, l: (0, i, 0))
    return pl.pallas_call(
        _in_proj_kernel,
        grid_spec=pltpu.PrefetchScalarGridSpec(
            num_scalar_prefetch=1,
            grid=(t // tm,),
            in_specs=[row(D_MODEL)] + [_layer_spec(w.shape) for w in weights],
            out_specs=[row(D_MODEL), row(2 * D_MODEL), slabs(3 * N_GROUPS), slabs(2 * H_C),
                       pl.BlockSpec((W_C, tm), lambda i, l: (0, i))],
            scratch_shapes=[pltpu.VMEM((tm, W_A), _BF16)],
        ),
        out_shape=[jax.ShapeDtypeStruct((t, D_MODEL), _F32), jax.ShapeDtypeStruct((t, 2 * D_MODEL), _F32),
                   jax.ShapeDtypeStruct((3 * N_GROUPS, t, LANES), _F32),
                   jax.ShapeDtypeStruct((2 * H_C, t, LANES), _BF16),
                   jax.ShapeDtypeStruct((W_C, t), _BF16)],
        compiler_params=pltpu.CompilerParams(dimension_semantics=("parallel",),
                                             vmem_limit_bytes=VMEM_LIMIT),
        name="in_proj",
    )(l, x, *weights)


def _dilated_kernel(q_ref, k_ref, v_ref, bias_ref, o_ref, lse_ref, *, dil, tq, kw, starts, variants, group):
    lane = lax.broadcasted_iota(jnp.int32, (1, LANES), 1)
    first = lane < HEAD_DIM
    head_mask = (first.astype(_BF16), (~first).astype(_BF16))
    heads = range(HEADS_PER_GROUP)
    tiles = [(r, t, ks, var) for r in range(dil) for t, (ks, var) in enumerate(zip(starts, variants))]
    for g0 in range(0, len(tiles), group):
        probs = tiles[g0:g0 + group]
        rows, qs, ks_, vs = [], [], [], []
        for r, t, ks, _ in probs:
            q_rows = pl.ds(r + t * tq * dil, tq, stride=dil)
            k_rows = pl.ds(r + ks * dil, kw, stride=dil)
            rows.append(q_rows)
            qs.append(q_ref[0, q_rows, :].astype(_BF16))
            ks_.append(k_ref[0, k_rows, :].astype(_BF16))
            vs.append(v_ref[0, k_rows, :].astype(_BF16))
        s = [[lax.dot_general(q * head_mask[hh], k, (((1,), (1,)), ((), ())),
                              preferred_element_type=_F32) + bias_ref[hh, p[3]] for hh in heads]
             for q, k, p in zip(qs, ks_, probs)]
        m = [[jnp.max(x, axis=-1, keepdims=True) for x in sp] for sp in s]
        e = [[jnp.exp(x - mx) for x, mx in zip(sp, mp)] for sp, mp in zip(s, m)]
        den = [[jnp.sum(x, axis=-1, keepdims=True) for x in ep] for ep in e]
        o = [[jnp.dot(x.astype(_BF16), v, preferred_element_type=_F32) * (1.0 / d) for x, d in zip(ep, dp)]
             for ep, dp, v in zip(e, den, vs)]
        for q_rows, op, mp, dp in zip(rows, o, m, den):
            o_ref[0, q_rows, :] = jnp.where(first, op[0], op[1])
            lse_ref[0, q_rows, :] = jnp.where(first, mp[0] + jnp.log(dp[0]), mp[1] + jnp.log(dp[1]))


def _dilated_attention(zb, bias, group, dil, batch, seq):
    tq, kw, starts, _, variants = _band_tiling(seq // dil)
    out = jax.ShapeDtypeStruct((batch, seq, LANES), _F32)
    o, lse = pl.pallas_call(
        functools.partial(_dilated_kernel, dil=dil, tq=tq, kw=kw, starts=starts, variants=variants,
                          group=max(1, DILATED_GROUP_ROWS // tq)),
        grid=(batch,),
        in_specs=[pl.BlockSpec((1, seq, LANES), lambda b: (3 * group, b, 0)),
                  pl.BlockSpec((1, seq, LANES), lambda b: (3 * group + 1, b, 0)),
                  pl.BlockSpec((1, seq, LANES), lambda b: (3 * group + 2, b, 0)),
                  pl.BlockSpec(bias.shape, lambda b: (0, 0, 0, 0))],
        out_specs=[pl.BlockSpec((1, seq, LANES), lambda b: (b, 0, 0))] * 2,
        out_shape=[out, out],
        compiler_params=pltpu.CompilerParams(dimension_semantics=("parallel",),
                                             vmem_limit_bytes=VMEM_LIMIT),
        name=f"dilated_d{dil}",
    )(zb, zb, zb, bias)
    return o.reshape(batch * seq, LANES), lse.reshape(batch * seq, LANES)


def _sublane_group_max(s):
    parts = [s[i:i + 8] for i in range(0, s.shape[0], 8)]
    while len(parts) > 1:
        parts = [jnp.maximum(a, b) for a, b in zip(parts[0::2], parts[1::2])] + parts[len(parts) & ~1:]
    return parts[0]


def _diff_kernel(l_ref, lam_init_ref, q_ref, k_ref, vt_ref, bias_ref, lq1_ref, lk1_ref, lq2_ref, lk2_ref,
                 sg_ref, o_ref, s0_scr, s1_scr, m_scr):
    seq = k_ref.shape[1]
    tq = q_ref.shape[1]
    kc = min(KEY_CHUNK, seq)
    s_scr = (s0_scr, s1_scr)

    @pl.when(pl.program_id(0) == 0)
    def _():
        s0_scr[...] = jnp.zeros_like(s0_scr)
        s1_scr[...] = jnp.zeros_like(s1_scr)
        m_scr[...] = jnp.zeros_like(m_scr)

    lam_init = lam_init_ref[l_ref[0]]
    lam = (jnp.exp(jnp.sum(lq1_ref[0] * lk1_ref[0], axis=-1, keepdims=True))
           - jnp.exp(jnp.sum(lq2_ref[0] * lk2_ref[0], axis=-1, keepdims=True)) + lam_init)
    first = lax.broadcasted_iota(jnp.int32, (1, LANES), 1) < HEAD_DIM
    q = q_ref[0]
    qm = [q * jnp.where(mask, 1.0, 0.0).astype(_BF16) for mask in (first, ~first)]
    ones = jnp.ones((ONES_ROWS, kc), _BF16)
    chunks = [slice(j, j + kc) for j in range(0, seq, kc)]
    m_prev = [m_scr[c, 0:1, :] for c in range(2)]

    def value_chunk(c, rows, acc):
        p = jnp.exp2(s_scr[c][rows, :] - m_prev[c]).astype(_BF16)
        d = jnp.dot(jnp.concatenate([vt_ref[:, rows], ones], axis=0), p, preferred_element_type=_F32)
        return d if acc is None else acc + d

    def logits_chunk(c, rows, m8):
        s = lax.dot_general(k_ref[0, rows, :], qm[c], (((1,), (1,)), ((), ())),
                            preferred_element_type=_F32) + bias_ref[0, rows, :]
        s_scr[c][rows, :] = s
        cm = _sublane_group_max(s)
        return cm if m8 is None else jnp.maximum(m8, cm)

    outs = []
    for c in range(2):
        m8 = None
        accs = [None, None]
        for j, rows in enumerate(chunks):
            accs[j % 2] = value_chunk(c, rows, accs[j % 2])
            m8 = logits_chunk(c, rows, m8)
        m_scr[c] = jnp.broadcast_to(jnp.max(m8, axis=0, keepdims=True), (8, tq))
        acc = accs[0] if accs[1] is None else accs[0] + accs[1]
        outs.append(acc[:LANES] * (1.0 / acc[LANES:LANES + 1]))
    o_t = outs[0] - lam * outs[1]
    y_t = o_t * lax.rsqrt(jnp.mean(o_t * o_t, axis=0, keepdims=True) + RMS_EPS)
    o_ref[0] = ((y_t.T * sg_ref[0]) * (1.0 - lam_init)).astype(o_ref.dtype)


def _diff_attention(l, lam_init, qkc, vct, bias_t, p, batch, seq):
    tq = min(seq, DIFF_LOGITS_ELEMS // seq)
    nq = seq // tq
    n_tiles = H_C * nq * batch

    def tile(n):
        return n // (nq * batch), (n // batch) % nq, n % batch

    cur = lambda n: tile(jnp.minimum(n, n_tiles - 1))
    prev = lambda n: tile(jnp.maximum(n - 1, 0))

    def q_map(n, l):
        h, i, b = cur(n)
        return h, b * nq + i, 0

    def k_map(n, l):
        h, _, b = cur(n)
        return H_C + h, b, 0

    def bias_map(n, l):
        h, i, _ = cur(n)
        return h, 0, i

    def vt_map(n, l):
        h, _, b = prev(n)
        return h, b

    def out_map(n, l):
        h, i, b = prev(n)
        return h, b * nq + i, 0

    vec = lambda w: pl.BlockSpec((1, 1, w), lambda n, l: (l[0], 0, 0))
    return pl.pallas_call(
        _diff_kernel,
        grid_spec=pltpu.PrefetchScalarGridSpec(
            num_scalar_prefetch=1,
            grid=(n_tiles + 1,),
            in_specs=[pl.BlockSpec(memory_space=pltpu.SMEM),
                      pl.BlockSpec((1, tq, LANES), q_map),
                      pl.BlockSpec((1, seq, LANES), k_map),
                      pl.BlockSpec((LANES, seq), vt_map),
                      pl.BlockSpec((1, seq, tq), bias_map),
                      vec(HEAD_DIM), vec(HEAD_DIM), vec(HEAD_DIM), vec(HEAD_DIM), vec(LANES)],
            out_specs=pl.BlockSpec((1, tq, LANES), out_map),
            scratch_shapes=[pltpu.VMEM((seq, tq), _F32), pltpu.VMEM((seq, tq), _F32),
                            pltpu.VMEM((2, 8, tq), _F32)],
        ),
        out_shape=jax.ShapeDtypeStruct((H_C, batch * seq, LANES), _BF16),
        compiler_params=pltpu.CompilerParams(dimension_semantics=("arbitrary",),
                                             vmem_limit_bytes=VMEM_LIMIT),
        name="diff_attn",
    )(l, lam_init, qkc, qkc, vct, bias_t, p["lam_q1"], p["lam_k1"], p["lam_q2"], p["lam_k2"], p["subln_g"])


def _merge_ffn_kernel(l_ref, x_ref, ma_ref, g12_ref, o0_ref, o1_ref, o2_ref, s0_ref, s1_ref, s2_ref, c_ref,
                      wpb_ref, wpc_ref, wo_ref, n2g_ref, wgu_ref, wdn_ref, out_ref):
    del l_ref
    lses = (s0_ref[...], s1_ref[...], s2_ref[...])
    m = jnp.maximum(jnp.maximum(lses[0], lses[1]), lses[2])
    es = [jnp.exp(s - m) for s in lses]
    inv = 1.0 / (es[0] + es[1] + es[2])
    b_out = jnp.concatenate([o[...] * (e * inv) for o, e in zip((o0_ref, o1_ref, o2_ref), es)], axis=-1)
    pb = jnp.dot(b_out.astype(_BF16), wpb_ref[0], preferred_element_type=_F32)
    c_out = jnp.concatenate([c_ref[hh] for hh in range(H_C)], axis=-1)
    pc = jnp.dot(c_out, wpc_ref[0], preferred_element_type=_F32)
    g12 = g12_ref[...]
    merged = ma_ref[...] + g12[:, :D_MODEL] * pb + g12[:, D_MODEL:] * pc
    x1 = x_ref[...] + jnp.dot(merged.astype(_BF16), wo_ref[0], preferred_element_type=_F32)

    h2 = ((x1 * lax.rsqrt(jnp.mean(x1 * x1, axis=-1, keepdims=True) + RMS_EPS)) * n2g_ref[0]).astype(_BF16)
    acc = x1
    for j in range(D_FF // FF_CHUNK):
        lo = j * FF_CHUNK
        g = jnp.dot(h2, wgu_ref[0, :, lo:lo + FF_CHUNK], preferred_element_type=_F32)
        u = jnp.dot(h2, wgu_ref[0, :, D_FF + lo:D_FF + lo + FF_CHUNK], preferred_element_type=_F32)
        act = ((g * _sigmoid(g)) * u).astype(_BF16)
        acc = acc + jnp.dot(act, wdn_ref[0, lo:lo + FF_CHUNK, :], preferred_element_type=_F32)
    out_ref[...] = acc


def _merge_ffn(l, x, ma, g12, o_groups, lse_groups, c_out, p, in_place):
    t = x.shape[0]
    tm = min(512, t)
    row = lambda w: pl.BlockSpec((tm, w), lambda i, l: (i, 0))
    weights = (p["w_pb"], p["w_pc"], p["w_o"], p["norm2_g"], p["w_gu"], p["w_down"])
    acts = (x, ma, g12) + tuple(o_groups) + tuple(lse_groups)
    return pl.pallas_call(
        _merge_ffn_kernel,
        input_output_aliases={1: 0} if in_place else {},
        grid_spec=pltpu.PrefetchScalarGridSpec(
            num_scalar_prefetch=1,
            grid=(t // tm,),
            in_specs=[row(a.shape[1]) for a in acts]
            + [pl.BlockSpec((H_C, tm, LANES), lambda i, l: (0, i, 0))]
            + [_layer_spec(w.shape) for w in weights],
            out_specs=row(D_MODEL),
        ),
        out_shape=jax.ShapeDtypeStruct((t, D_MODEL), _F32),
        compiler_params=pltpu.CompilerParams(dimension_semantics=("parallel",),
                                             vmem_limit_bytes=VMEM_LIMIT),
        name="merge_ffn",
    )(l, *acts, c_out, *weights)


def _layer(l, x, p, biases, lam_init, batch, seq, in_place):
    ma, g12, zb, qkc, vct = _in_proj(l, x, p)
    o_groups, lse_groups = [], []
    for g, (_, dil) in enumerate(DILATED_GROUPS):
        o, lse = _dilated_attention(zb, biases["b"][g], g, dil, batch, seq)
        o_groups.append(o)
        lse_groups.append(lse)
    c_out = _diff_attention(l, lam_init, qkc, vct, biases["c"], p, batch, seq)
    return _merge_ffn(l, x, ma, g12, o_groups, lse_groups, c_out, p, in_place)


def _prepare_params(p):
    depth = p["w_in"].shape[0]
    row = lambda a: a.reshape(depth, 1, a.shape[-1])
    tile2 = lambda a: row(jnp.tile(a, (1, LANES // HEAD_DIM)))
    sgu_b = jnp.broadcast_to(jnp.swapaxes(p["sgu_b"], 1, 2)[..., None], (depth, CHUNK, A_GROUPS, HEAD_DIM))
    return dict(
        norm1_g=row(p["norm1_g"]), norm2_g=row(p["norm2_g"]),
        w_in=jnp.take(p["w_in"], _in_proj_permutation(), axis=2).astype(_BF16),
        sgu_ln_g=row(p["sgu_ln_g"]), sgu_ln_b=row(p["sgu_ln_b"]),
        sgu_w=p["sgu_w"].astype(_BF16), sgu_b=sgu_b.reshape(depth, CHUNK, W_A),
        qn_b=tile2(p["qn_b"]), kn_b=tile2(p["kn_b"]), qn_c=tile2(p["qn_c"]), kn_c=tile2(p["kn_c"]),
        lam_q1=row(p["lam_q1"]), lam_k1=row(p["lam_k1"]), lam_q2=row(p["lam_q2"]), lam_k2=row(p["lam_k2"]),
        subln_g=row(p["subln_g"]),
        w_pa=p["w_pa"].astype(_BF16), w_pb=p["w_pb"].astype(_BF16), w_pc=p["w_pc"].astype(_BF16),
        w_o=p["w_o"].astype(_BF16), w_gu=p["w_gu"].astype(_BF16), w_down=p["w_down"].astype(_BF16),
    )


def _biases(rel_bias, seq):
    return dict(c=_bias_c(rel_bias, seq),
                b=[_bias_b(rel_bias, g, dil, seq // dil) for g, (_, dil) in enumerate(DILATED_GROUPS)])


@jax.jit
def _trunk(x_prompt, x_sample, params):
    p = _prepare_params(params)
    depth = params["w_in"].shape[0]
    lam_init = jnp.asarray([0.8 - 0.6 * math.exp(-0.3 * l) for l in range(depth)], _F32)
    groups = [(x.shape[0], x.shape[1]) for x in (x_prompt, x_sample)]
    biases = [_biases(params["rel_bias"], seq) for _, seq in groups]

    def layer(l, xs, in_place):
        lidx = jnp.reshape(l, (1,)).astype(jnp.int32)
        return tuple(_layer(lidx, x, p, b, lam_init, batch, seq, in_place)
                     for x, b, (batch, seq) in zip(xs, biases, groups))

    xs = tuple(x.reshape(-1, D_MODEL) for x in (x_prompt, x_sample))
    xs = layer(jnp.int32(0), xs, in_place=False)
    ys = lax.fori_loop(1, depth, functools.partial(layer, in_place=True), xs)
    return tuple(y.reshape(x.shape) for y, x in zip(ys, (x_prompt, x_sample)))


def kernel(x_prompt, x_sample, rel_bias, norm1_g, w_in, sgu_ln_g, sgu_ln_b, sgu_w, sgu_b, qn_b, kn_b, qn_c, kn_c, lam_q1, lam_k1, lam_q2, lam_k2, subln_g, w_pa, w_pb, w_pc, w_o, norm2_g, w_gu, w_down):
    params = dict(rel_bias=rel_bias, norm1_g=norm1_g, w_in=w_in, sgu_ln_g=sgu_ln_g, sgu_ln_b=sgu_ln_b,
                  sgu_w=sgu_w, sgu_b=sgu_b, qn_b=qn_b, kn_b=kn_b, qn_c=qn_c, kn_c=kn_c,
                  lam_q1=lam_q1, lam_k1=lam_k1, lam_q2=lam_q2, lam_k2=lam_k2, subln_g=subln_g,
                  w_pa=w_pa, w_pb=w_pb, w_pc=w_pc, w_o=w_o, norm2_g=norm2_g, w_gu=w_gu, w_down=w_down)
    return _trunk(x_prompt, x_sample, params)
```

```python
import functools
import math

import numpy as np
import jax
import jax.numpy as jnp
from jax import lax
from jax.experimental import pallas as pl
from jax.experimental.pallas import tpu as pltpu

D_MODEL = 1024
HEAD_DIM = 64
LANES = 128
CHUNK = 128
A_GROUPS = 8
W_A = A_GROUPS * HEAD_DIM
DILATED_GROUPS = ((128, 1), (512, 4), (2048, 16))
HEADS_PER_GROUP = 2
N_GROUPS = len(DILATED_GROUPS)
H_B = HEADS_PER_GROUP * N_GROUPS
W_B = H_B * HEAD_DIM
H_C = 4
W_C_QK = 2 * H_C * HEAD_DIM
W_C = H_C * 2 * HEAD_DIM
NUM_BUCKETS = 32
REL_MAX_DIST = 1024
D_FF = ((8 * D_MODEL + 3 * 256 - 1) // (3 * 256)) * 256
N_BRANCH = 3
RMS_EPS = 1e-6
LN_EPS = 1e-5
NEG_INF = -1e30
BAND = 64
assert all(w // (2 * d) == BAND for w, d in DILATED_GROUPS)
assert HEADS_PER_GROUP * HEAD_DIM == LANES

OFF_A = 0
OFF_B = 2 * W_A
OFF_C = OFF_B + 3 * W_B
OFF_G = OFF_C + 2 * W_C_QK + W_C
IN_COLS = OFF_G + N_BRANCH * D_MODEL
GRP_COLS = 3 * LANES

FF_CHUNK = 256
DILATED_GROUP_ROWS = 1024
KEY_CHUNK = 256
DIFF_LOGITS_ELEMS = 2 ** 21
ONES_ROWS = 16
LOG2E = math.log2(math.e)
VMEM_LIMIT = 56 * 2 ** 20

_F32 = jnp.float32
_BF16 = jnp.bfloat16


def _in_proj_permutation():
    cols = list(range(OFF_B))
    for g in range(N_GROUPS):
        for part in range(3):
            base = OFF_B + part * W_B + g * LANES
            cols += list(range(base, base + LANES))
    for part in range(2):
        for h in range(H_C):
            for c in range(2):
                base = OFF_C + part * W_C_QK + c * (H_C * HEAD_DIM) + h * HEAD_DIM
                cols += list(range(base, base + HEAD_DIM))
    cols += list(range(OFF_C + 2 * W_C_QK, IN_COLS))
    assert sorted(cols) == list(range(IN_COLS))
    runs, start = [], 0
    for i in range(1, IN_COLS + 1):
        if i == IN_COLS or cols[i] != cols[i - 1] + 1:
            runs.append((cols[start], cols[i - 1] + 1))
            start = i
    return runs


def _bucket_thresholds(max_dist):
    nb = NUM_BUCKETS // 2
    max_exact = nb // 2
    n = np.arange(1, max_dist + 1)

    def idx(dtype):
        nf = n.astype(dtype)
        large = max_exact + (np.log(nf / dtype(max_exact)) / dtype(math.log(REL_MAX_DIST / max_exact))
                             * dtype(nb - max_exact)).astype(np.int32)
        return np.where(n < max_exact, n, np.minimum(large, nb - 1))

    i32, i64 = idx(np.float32), idx(np.float64)
    assert np.array_equal(i32, i64) and np.all(np.diff(i32) >= 0) and np.all(np.diff(i32) <= 1)
    thr = [0] + [int(n[np.argmax(i32 >= b)]) for b in range(1, nb)]
    return thr


_THR = _bucket_thresholds(1 << 16)


def _bias_from_rel(rel, tab_ref, head):
    nb = NUM_BUCKETS // 2
    val = jnp.full(rel.shape, tab_ref[nb - 1, head], _F32)
    for b in range(nb - 2, -1, -1):
        val = jnp.where(rel >= -(_THR[b + 1] - 1), tab_ref[b, head], val)
    for b in range(1, nb):
        val = jnp.where(rel >= _THR[b], tab_ref[nb + b, head], val)
    return val


def _bias_c_kernel(tab_ref, out_ref, band_scr):
    h = pl.program_id(0)
    j = pl.program_id(1)
    _, s, tc = out_ref.shape

    @pl.when(j == 0)
    def _():
        def fill(i, carry):
            y0 = pl.multiple_of(i * tc, tc)
            y = lax.broadcasted_iota(jnp.int32, (tc, tc), 0) + y0
            c = lax.broadcasted_iota(jnp.int32, (tc, tc), 1)
            band_scr[pl.ds(y0, tc), :] = _bias_from_rel(y - (s - tc) - c, tab_ref, H_B + h) * LOG2E
            return carry

        lax.fori_loop(0, band_scr.shape[0] // tc, fill, 0)

    out_ref[0] = band_scr[pl.ds(pl.multiple_of((s - tc) - j * tc, tc), s), :]


def _bias_c(rel_bias, s):
    tc = min(256, s)
    return pl.pallas_call(
        _bias_c_kernel,
        grid=(H_C, s // tc),
        in_specs=[pl.BlockSpec(memory_space=pltpu.SMEM)],
        out_specs=pl.BlockSpec((1, s, tc), lambda h, j: (h, 0, j)),
        out_shape=jax.ShapeDtypeStruct((H_C, s, s), _F32),
        scratch_shapes=[pltpu.VMEM((2 * s - tc, tc), _F32)],
        compiler_params=pltpu.CompilerParams(dimension_semantics=("arbitrary", "arbitrary")),
        name="bias_c",
    )(rel_bias)


def _band_tiling(length):
    tq = min(256, length)
    kw = min(tq + 2 * BAND, length)
    nt = length // tq
    assert nt * tq == length
    starts = [min(max(t * tq - BAND, 0), length - kw) for t in range(nt)]
    offs = sorted({starts[t] - t * tq for t in range(nt)})
    variants = [offs.index(starts[t] - t * tq) for t in range(nt)]
    return tq, kw, starts, offs, variants


def _bias_b_kernel(tab_ref, out_ref, *, group, dil, offs):
    _, _, tq, kw = out_ref.shape
    i = lax.broadcasted_iota(jnp.int32, (tq, kw), 0)
    j = lax.broadcasted_iota(jnp.int32, (tq, kw), 1)
    for v, off in enumerate(offs):
        dist = j + off - i
        inside = jnp.abs(dist) <= BAND
        for hh in range(HEADS_PER_GROUP):
            val = _bias_from_rel(dist * dil, tab_ref, group * HEADS_PER_GROUP + hh)
            out_ref[hh, v] = jnp.where(inside, val, NEG_INF)


def _bias_b(rel_bias, group, dil, length):
    tq, kw, _, offs, _ = _band_tiling(length)
    return pl.pallas_call(
        functools.partial(_bias_b_kernel, group=group, dil=dil, offs=offs),
        in_specs=[pl.BlockSpec(memory_space=pltpu.SMEM)],
        out_shape=jax.ShapeDtypeStruct((HEADS_PER_GROUP, len(offs), tq, kw), _F32),
        name=f"bias_b{group}",
    )(rel_bias)


def _head_rms(z, gain):
    lane = lax.broadcasted_iota(jnp.int32, (1, LANES), 1)
    first = lane < HEAD_DIM
    sq = z * z
    s0 = jnp.sum(jnp.where(first, sq, 0.0), axis=-1, keepdims=True)
    s1 = jnp.sum(jnp.where(first, 0.0, sq), axis=-1, keepdims=True)
    ms = jnp.where(first, s0, s1) * (1.0 / HEAD_DIM)
    return (z * lax.rsqrt(ms + RMS_EPS)) * gain


def _sigmoid(x):
    return 1.0 / (1.0 + jnp.exp(-x))


def _gelu_tanh(x):
    c = math.sqrt(2.0 / math.pi)
    return x * (0.5 * (1.0 + jnp.tanh(c * (x + 0.044715 * (x * x * x)))))


def _in_proj_kernel(l_ref, x_ref, n1g_ref, win_ref, lng_ref, lnb_ref, sguw_ref, sgub_ref,
                    qgb_ref, kgb_ref, qgc_ref, kgc_ref, wpa_ref,
                    ma_ref, g12_ref, zb_ref, qkc_ref, vct_ref, a_scr):
    del l_ref
    tm = x_ref.shape[0]
    first = lax.broadcasted_iota(jnp.int32, (1, LANES), 1) < HEAD_DIM

    def project(rows):
        x = x_ref[rows, :]
        h = ((x * lax.rsqrt(jnp.mean(x * x, axis=-1, keepdims=True) + RMS_EPS)) * n1g_ref[0]).astype(_BF16)
        proj = lambda lo, hi: jnp.dot(h, win_ref[0, :, lo:hi], preferred_element_type=_F32)
        za = proj(OFF_A, OFF_B)
        gates = _sigmoid(proj(OFF_G, IN_COLS))
        g12_ref[rows, :] = gates[:, D_MODEL:]
        return za, gates[:, :D_MODEL], proj(OFF_B, OFF_C), proj(OFF_C, OFF_G)

    def mix(rows, za, gate0, zb, zc):
        ga = _gelu_tanh(za)
        u, v = ga[:, :W_A], ga[:, W_A:]
        vc = v - jnp.mean(v, axis=-1, keepdims=True)
        var = jnp.mean(vc * vc, axis=-1, keepdims=True)
        vn = ((vc * lax.rsqrt(var + LN_EPS)) * lng_ref[0] + lnb_ref[0]).astype(_BF16)
        for c in range(za.shape[0] // CHUNK):
            chunk = slice(c * CHUNK, (c + 1) * CHUNK)
            dst = slice(rows.start + c * CHUNK, rows.start + (c + 1) * CHUNK)
            for j in range(A_GROUPS // 2):
                cols = slice(j * LANES, (j + 1) * LANES)
                slab = vn[chunk, cols]
                r0 = jnp.dot(sguw_ref[0, 2 * j], slab, preferred_element_type=_F32)
                r1 = jnp.dot(sguw_ref[0, 2 * j + 1], slab, preferred_element_type=_F32)
                sv = jnp.where(first, r0, r1) + sgub_ref[0, :, cols]
                a_scr[dst, cols] = (u[chunk, cols] * sv).astype(_BF16)
        ma_ref[rows, :] = gate0 * jnp.dot(a_scr[rows, :], wpa_ref[0], preferred_element_type=_F32)

        for g in range(N_GROUPS):
            base = g * GRP_COLS
            zb_ref[3 * g, rows, :] = _head_rms(zb[:, base:base + LANES], qgb_ref[0]) * HEAD_DIM ** -0.5
            zb_ref[3 * g + 1, rows, :] = _head_rms(zb[:, base + LANES:base + 2 * LANES], kgb_ref[0])
            zb_ref[3 * g + 2, rows, :] = zb[:, base + 2 * LANES:base + 3 * LANES]

        for hh in range(H_C):
            cols = slice(hh * LANES, (hh + 1) * LANES)
            k_cols = slice(W_C_QK + hh * LANES, W_C_QK + (hh + 1) * LANES)
            qkc_ref[hh, rows, :] = (_head_rms(zc[:, cols], qgc_ref[0]) * (HEAD_DIM ** -0.5 * LOG2E)).astype(_BF16)
            qkc_ref[H_C + hh, rows, :] = _head_rms(zc[:, k_cols], kgc_ref[0]).astype(_BF16)
        vct_ref[:, rows] = zc[:, 2 * W_C_QK:].T.astype(_BF16)

    n_split = 2 if tm % (2 * LANES) == 0 else 1
    parts = [slice(i * tm // n_split, (i + 1) * tm // n_split) for i in range(n_split)]
    projected = [project(rows) for rows in parts]
    for rows, vals in zip(parts, projected):
        mix(rows, *vals)


def _layer_spec(shape):
    nd = len(shape)
    return pl.BlockSpec((1,) + tuple(shape[1:]), lambda i, l: (l[0],) + (0,) * (nd - 1),
                        pipeline_mode=pl.Buffered(1))


def _in_proj(l, x, p):
    t = x.shape[0]
    tm = min(512, t)
    row = lambda w: pl.BlockSpec((tm, w), lambda i, l: (i, 0))
    weights = (p["norm1_g"], p["w_in"], p["sgu_ln_g"], p["sgu_ln_b"], p["sgu_w"], p["sgu_b"],
               p["qn_b"], p["kn_b"], p["qn_c"], p["kn_c"], p["w_pa"])
    slabs = lambda n: pl.BlockSpec((n, tm, LANES), lambda i, l: (0, i, 0))
    return pl.pallas_call(
        _in_proj_kernel,
        grid_spec=pltpu.PrefetchScalarGridSpec(
            num_scalar_prefetch=1,
            grid=(t // tm,),
            in_specs=[row(D_MODEL)] + [_layer_spec(w.shape) for w in weights],
            out_specs=[row(D_MODEL), row(2 * D_MODEL), slabs(3 * N_GROUPS), slabs(2 * H_C),
                       pl.BlockSpec((W_C, tm), lambda i, l: (0, i))],
            scratch_shapes=[pltpu.VMEM((tm, W_A), _BF16)],
        ),
        out_shape=[jax.ShapeDtypeStruct((t, D_MODEL), _F32), jax.ShapeDtypeStruct((t, 2 * D_MODEL), _F32),
                   jax.ShapeDtypeStruct((3 * N_GROUPS, t, LANES), _F32),
                   jax.ShapeDtypeStruct((2 * H_C, t, LANES), _BF16),
                   jax.ShapeDtypeStruct((W_C, t), _BF16)],
        compiler_params=pltpu.CompilerParams(dimension_semantics=("parallel",),
                                             vmem_limit_bytes=VMEM_LIMIT),
        name="in_proj",
    )(l, x, *weights)


def _dilated_kernel(q_ref, k_ref, v_ref, bias_ref, o_ref, lse_ref, *, dil, tq, kw, starts, variants, group):
    lane = lax.broadcasted_iota(jnp.int32, (1, LANES), 1)
    first = lane < HEAD_DIM
    head_mask = (first.astype(_BF16), (~first).astype(_BF16))
    heads = range(HEADS_PER_GROUP)
    tiles = [(r, t, ks, var) for r in range(dil) for t, (ks, var) in enumerate(zip(starts, variants))]
    for g0 in range(0, len(tiles), group):
        probs = tiles[g0:g0 + group]
        rows, qs, ks_, vs = [], [], [], []
        for r, t, ks, _ in probs:
            q_rows = pl.ds(r + t * tq * dil, tq, stride=dil)
            k_rows = pl.ds(r + ks * dil, kw, stride=dil)
            rows.append(q_rows)
            qs.append(q_ref[0, q_rows, :].astype(_BF16))
            ks_.append(k_ref[0, k_rows, :].astype(_BF16))
            vs.append(v_ref[0, k_rows, :].astype(_BF16))
        s = [[lax.dot_general(q * head_mask[hh], k, (((1,), (1,)), ((), ())),
                              preferred_element_type=_F32) + bias_ref[hh, p[3]] for hh in heads]
             for q, k, p in zip(qs, ks_, probs)]
        m = [[jnp.max(x, axis=-1, keepdims=True) for x in sp] for sp in s]
        e = [[jnp.exp(x - mx) for x, mx in zip(sp, mp)] for sp, mp in zip(s, m)]
        den = [[jnp.sum(x, axis=-1, keepdims=True) for x in ep] for ep in e]
        o = [[jnp.dot(x.astype(_BF16), v, preferred_element_type=_F32) * (1.0 / d) for x, d in zip(ep, dp)]
             for ep, dp, v in zip(e, den, vs)]
        for q_rows, op, mp, dp in zip(rows, o, m, den):
            o_ref[0, q_rows, :] = jnp.where(first, op[0], op[1])
            lse_ref[0, q_rows, :] = jnp.where(first, mp[0] + jnp.log(dp[0]), mp[1] + jnp.log(dp[1]))


def _dilated_attention(zb, bias, group, dil, batch, seq):
    tq, kw, starts, _, variants = _band_tiling(seq // dil)
    out = jax.ShapeDtypeStruct((batch, seq, LANES), _F32)
    o, lse = pl.pallas_call(
        functools.partial(_dilated_kernel, dil=dil, tq=tq, kw=kw, starts=starts, variants=variants,
                          group=max(1, DILATED_GROUP_ROWS // tq)),
        grid=(batch,),
        in_specs=[pl.BlockSpec((1, seq, LANES), lambda b: (3 * group, b, 0)),
                  pl.BlockSpec((1, seq, LANES), lambda b: (3 * group + 1, b, 0)),
                  pl.BlockSpec((1, seq, LANES), lambda b: (3 * group + 2, b, 0)),
                  pl.BlockSpec(bias.shape, lambda b: (0, 0, 0, 0))],
        out_specs=[pl.BlockSpec((1, seq, LANES), lambda b: (b, 0, 0))] * 2,
        out_shape=[out, out],
        compiler_params=pltpu.CompilerParams(dimension_semantics=("parallel",),
                                             vmem_limit_bytes=VMEM_LIMIT),
        name=f"dilated_d{dil}",
    )(zb, zb, zb, bias)
    return o.reshape(batch * seq, LANES), lse.reshape(batch * seq, LANES)


def _sublane_group_max(s):
    parts = [s[i:i + 8] for i in range(0, s.shape[0], 8)]
    while len(parts) > 1:
        parts = [jnp.maximum(a, b) for a, b in zip(parts[0::2], parts[1::2])] + parts[len(parts) & ~1:]
    return parts[0]


def _diff_kernel(l_ref, lam_init_ref, q_ref, k_ref, vt_ref, bias_ref, lq1_ref, lk1_ref, lq2_ref, lk2_ref,
                 sg_ref, o_ref, s0_scr, s1_scr, m_scr):
    seq = k_ref.shape[1]
    tq = q_ref.shape[1]
    kc = min(KEY_CHUNK, seq)
    s_scr = (s0_scr, s1_scr)

    @pl.when(pl.program_id(0) == 0)
    def _():
        s0_scr[...] = jnp.zeros_like(s0_scr)
        s1_scr[...] = jnp.zeros_like(s1_scr)
        m_scr[...] = jnp.zeros_like(m_scr)

    lam_init = lam_init_ref[l_ref[0]]
    lam = (jnp.exp(jnp.sum(lq1_ref[0] * lk1_ref[0], axis=-1, keepdims=True))
           - jnp.exp(jnp.sum(lq2_ref[0] * lk2_ref[0], axis=-1, keepdims=True)) + lam_init)
    first = lax.broadcasted_iota(jnp.int32, (1, LANES), 1) < HEAD_DIM
    q = q_ref[0]
    qm = [q * jnp.where(mask, 1.0, 0.0).astype(_BF16) for mask in (first, ~first)]
    ones = jnp.ones((ONES_ROWS, kc), _BF16)
    chunks = [slice(j, j + kc) for j in range(0, seq, kc)]
    m_prev = [m_scr[c, 0:1, :] for c in range(2)]

    def value_chunk(c, rows, acc):
        p = jnp.exp2(s_scr[c][rows, :] - m_prev[c]).astype(_BF16)
        d = jnp.dot(jnp.concatenate([vt_ref[:, rows], ones], axis=0), p, preferred_element_type=_F32)
        return d if acc is None else acc + d

    def logits_chunk(c, rows, m8):
        s = lax.dot_general(k_ref[0, rows, :], qm[c], (((1,), (1,)), ((), ())),
                            preferred_element_type=_F32) + bias_ref[0, rows, :]
        s_scr[c][rows, :] = s
        cm = _sublane_group_max(s)
        return cm if m8 is None else jnp.maximum(m8, cm)

    outs = []
    for c in range(2):
        m8 = None
        accs = [None, None]
        for j, rows in enumerate(chunks):
            accs[j % 2] = value_chunk(c, rows, accs[j % 2])
            m8 = logits_chunk(c, rows, m8)
        m_scr[c] = jnp.broadcast_to(jnp.max(m8, axis=0, keepdims=True), (8, tq))
        acc = accs[0] if accs[1] is None else accs[0] + accs[1]
        outs.append(acc[:LANES] * (1.0 / acc[LANES:LANES + 1]))
    o_t = outs[0] - lam * outs[1]
    y_t = o_t * lax.rsqrt(jnp.mean(o_t * o_t, axis=0, keepdims=True) + RMS_EPS)
    o_ref[0] = ((y_t.T * sg_ref[0]) * (1.0 - lam_init)).astype(o_ref.dtype)


def _diff_attention(l, lam_init, qkc, vct, bias_t, p, batch, seq):
    tq = min(seq, DIFF_LOGITS_ELEMS // seq)
    nq = seq // tq
    n_tiles = H_C * nq * batch

    def tile(n):
        return n // (nq * batch), (n // batch) % nq, n % batch

    cur = lambda n: tile(jnp.minimum(n, n_tiles - 1))
    prev = lambda n: tile(jnp.maximum(n - 1, 0))

    def q_map(n, l):
        h, i, b = cur(n)
        return h, b * nq + i, 0

    def k_map(n, l):
        h, _, b = cur(n)
        return H_C + h, b, 0

    def bias_map(n, l):
        h, i, _ = cur(n)
        return h, 0, i

    def vt_map(n, l):
        h, _, b = prev(n)
        return h, b

    def out_map(n, l):
        h, i, b = prev(n)
        return h, b * nq + i, 0

    vec = lambda w: pl.BlockSpec((1, 1, w), lambda n, l: (l[0], 0, 0))
    return pl.pallas_call(
        _diff_kernel,
        grid_spec=pltpu.PrefetchScalarGridSpec(
            num_scalar_prefetch=1,
            grid=(n_tiles + 1,),
            in_specs=[pl.BlockSpec(memory_space=pltpu.SMEM),
                      pl.BlockSpec((1, tq, LANES), q_map),
                      pl.BlockSpec((1, seq, LANES), k_map),
                      pl.BlockSpec((LANES, seq), vt_map),
                      pl.BlockSpec((1, seq, tq), bias_map),
                      vec(HEAD_DIM), vec(HEAD_DIM), vec(HEAD_DIM), vec(HEAD_DIM), vec(LANES)],
            out_specs=pl.BlockSpec((1, tq, LANES), out_map),
            scratch_shapes=[pltpu.VMEM((seq, tq), _F32), pltpu.VMEM((seq, tq), _F32),
                            pltpu.VMEM((2, 8, tq), _F32)],
        ),
        out_shape=jax.ShapeDtypeStruct((H_C, batch * seq, LANES), _BF16),
        compiler_params=pltpu.CompilerParams(dimension_semantics=("arbitrary",),
                                             vmem_limit_bytes=VMEM_LIMIT),
        name="diff_attn",
    )(l, lam_init, qkc, qkc, vct, bias_t, p["lam_q1"], p["lam_k1"], p["lam_q2"], p["lam_k2"], p["subln_g"])


def _merge_ffn_kernel(l_ref, x_ref, ma_ref, g12_ref, o0_ref, o1_ref, o2_ref, s0_ref, s1_ref, s2_ref, c_ref,
                      wpb_ref, wpc_ref, wo_ref, n2g_ref, wgu_ref, wdn_ref, out_ref):
    del l_ref
    lses = (s0_ref[...], s1_ref[...], s2_ref[...])
    m = jnp.maximum(jnp.maximum(lses[0], lses[1]), lses[2])
    es = [jnp.exp(s - m) for s in lses]
    inv = 1.0 / (es[0] + es[1] + es[2])
    b_out = jnp.concatenate([o[...] * (e * inv) for o, e in zip((o0_ref, o1_ref, o2_ref), es)], axis=-1)
    pb = jnp.dot(b_out.astype(_BF16), wpb_ref[0], preferred_element_type=_F32)
    c_out = jnp.concatenate([c_ref[hh] for hh in range(H_C)], axis=-1)
    pc = jnp.dot(c_out, wpc_ref[0], preferred_element_type=_F32)
    g12 = g12_ref[...]
    merged = ma_ref[...] + g12[:, :D_MODEL] * pb + g12[:, D_MODEL:] * pc
    x1 = x_ref[...] + jnp.dot(merged.astype(_BF16), wo_ref[0], preferred_element_type=_F32)

    h2 = ((x1 * lax.rsqrt(jnp.mean(x1 * x1, axis=-1, keepdims=True) + RMS_EPS)) * n2g_ref[0]).astype(_BF16)
    acc = x1
    for j in range(D_FF // FF_CHUNK):
        lo = j * FF_CHUNK
        g = jnp.dot(h2, wgu_ref[0, :, lo:lo + FF_CHUNK], preferred_element_type=_F32)
        u = jnp.dot(h2, wgu_ref[0, :, D_FF + lo:D_FF + lo + FF_CHUNK], preferred_element_type=_F32)
        act = ((g * _sigmoid(g)) * u).astype(_BF16)
        acc = acc + jnp.dot(act, wdn_ref[0, lo:lo + FF_CHUNK, :], preferred_element_type=_F32)
    out_ref[...] = acc


def _merge_ffn(l, x, ma, g12, o_groups, lse_groups, c_out, p, in_place):
    t = x.shape[0]
    tm = min(512, t)
    row = lambda w: pl.BlockSpec((tm, w), lambda i, l: (i, 0))
    weights = (p["w_pb"], p["w_pc"], p["w_o"], p["norm2_g"], p["w_gu"], p["w_down"])
    acts = (x, ma, g12) + tuple(o_groups) + tuple(lse_groups)
    return pl.pallas_call(
        _merge_ffn_kernel,
        input_output_aliases={1: 0} if in_place else {},
        grid_spec=pltpu.PrefetchScalarGridSpec(
            num_scalar_prefetch=1,
            grid=(t // tm,),
            in_specs=[row(a.shape[1]) for a in acts]
            + [pl.BlockSpec((H_C, tm, LANES), lambda i, l: (0, i, 0))]
            + [_layer_spec(w.shape) for w in weights],
            out_specs=row(D_MODEL),
        ),
        out_shape=jax.ShapeDtypeStruct((t, D_MODEL), _F32),
        compiler_params=pltpu.CompilerParams(dimension_semantics=("parallel",),
                                             vmem_limit_bytes=VMEM_LIMIT),
        name="merge_ffn",
    )(l, *acts, c_out, *weights)


def _layer(l, x, p, biases, lam_init, batch, seq, in_place):
    ma, g12, zb, qkc, vct = _in_proj(l, x, p)
    o_groups, lse_groups = [], []
    for g, (_, dil) in enumerate(DILATED_GROUPS):
        o, lse = _dilated_attention(zb, biases["b"][g], g, dil, batch, seq)
        o_groups.append(o)
        lse_groups.append(lse)
    c_out = _diff_attention(l, lam_init, qkc, vct, biases["c"], p, batch, seq)
    return _merge_ffn(l, x, ma, g12, o_groups, lse_groups, c_out, p, in_place)


def _prepare_params(p):
    depth = p["w_in"].shape[0]
    row = lambda a: a.reshape(depth, 1, a.shape[-1])
    tile2 = lambda a: row(jnp.tile(a, (1, LANES // HEAD_DIM)))
    sgu_b = jnp.broadcast_to(jnp.swapaxes(p["sgu_b"], 1, 2)[..., None], (depth, CHUNK, A_GROUPS, HEAD_DIM))
    return dict(
        norm1_g=row(p["norm1_g"]), norm2_g=row(p["norm2_g"]),
        w_in=jnp.concatenate([p["w_in"][:, :, a:b].astype(_BF16) for a, b in _in_proj_permutation()], axis=2),
        sgu_ln_g=row(p["sgu_ln_g"]), sgu_ln_b=row(p["sgu_ln_b"]),
        sgu_w=p["sgu_w"].astype(_BF16), sgu_b=sgu_b.reshape(depth, CHUNK, W_A),
        qn_b=tile2(p["qn_b"]), kn_b=tile2(p["kn_b"]), qn_c=tile2(p["qn_c"]), kn_c=tile2(p["kn_c"]),
        lam_q1=row(p["lam_q1"]), lam_k1=row(p["lam_k1"]), lam_q2=row(p["lam_q2"]), lam_k2=row(p["lam_k2"]),
        subln_g=row(p["subln_g"]),
        w_pa=p["w_pa"].astype(_BF16), w_pb=p["w_pb"].astype(_BF16), w_pc=p["w_pc"].astype(_BF16),
        w_o=p["w_o"].astype(_BF16), w_gu=p["w_gu"].astype(_BF16), w_down=p["w_down"].astype(_BF16),
    )


def _biases(rel_bias, seq):
    return dict(c=_bias_c(rel_bias, seq),
                b=[_bias_b(rel_bias, g, dil, seq // dil) for g, (_, dil) in enumerate(DILATED_GROUPS)])


@jax.jit
def _trunk(x_prompt, x_sample, params):
    p = _prepare_params(params)
    depth = params["w_in"].shape[0]
    lam_init = jnp.asarray([0.8 - 0.6 * math.exp(-0.3 * l) for l in range(depth)], _F32)
    groups = [(x.shape[0], x.shape[1]) for x in (x_prompt, x_sample)]
    biases = [_biases(params["rel_bias"], seq) for _, seq in groups]

    def layer(l, xs, in_place):
        lidx = jnp.reshape(l, (1,)).astype(jnp.int32)
        return tuple(_layer(lidx, x, p, b, lam_init, batch, seq, in_place)
                     for x, b, (batch, seq) in zip(xs, biases, groups))

    xs = tuple(x.reshape(-1, D_MODEL) for x in (x_prompt, x_sample))
    xs = layer(jnp.int32(0), xs, in_place=False)
    ys = lax.fori_loop(1, depth, functools.partial(layer, in_place=True), xs)
    return tuple(y.reshape(x.shape) for y, x in zip(ys, (x_prompt, x_sample)))


def kernel(x_prompt, x_sample, rel_bias, norm1_g, w_in, sgu_ln_g, sgu_ln_b, sgu_w, sgu_b, qn_b, kn_b, qn_c, kn_c, lam_q1, lam_k1, lam_q2, lam_k2, subln_g, w_pa, w_pb, w_pc, w_o, norm2_g, w_gu, w_down):
    params = dict(rel_bias=rel_bias, norm1_g=norm1_g, w_in=w_in, sgu_ln_g=sgu_ln_g, sgu_ln_b=sgu_ln_b,
                  sgu_w=sgu_w, sgu_b=sgu_b, qn_b=qn_b, kn_b=kn_b, qn_c=qn_c, kn_c=kn_c,
                  lam_q1=lam_q1, lam_k1=lam_k1, lam_q2=lam_q2, lam_k2=lam_k2, subln_g=subln_g,
                  w_pa=w_pa, w_pb=w_pb, w_pc=w_pc, w_o=w_o, norm2_g=norm2_g, w_gu=w_gu, w_down=w_down)
    return _trunk(x_prompt, x_sample, params)
```

```python
import functools
import math

import numpy as np
import jax
import jax.numpy as jnp
from jax import lax
from jax.experimental import pallas as pl
from jax.experimental.pallas import tpu as pltpu

D_MODEL = 1024
HEAD_DIM = 64
LANES = 128
CHUNK = 128
A_GROUPS = 8
W_A = A_GROUPS * HEAD_DIM
DILATED_GROUPS = ((128, 1), (512, 4), (2048, 16))
HEADS_PER_GROUP = 2
N_GROUPS = len(DILATED_GROUPS)
H_B = HEADS_PER_GROUP * N_GROUPS
W_B = H_B * HEAD_DIM
H_C = 4
W_C_QK = 2 * H_C * HEAD_DIM
W_C = H_C * 2 * HEAD_DIM
NUM_BUCKETS = 32
REL_MAX_DIST = 1024
D_FF = ((8 * D_MODEL + 3 * 256 - 1) // (3 * 256)) * 256
N_BRANCH = 3
RMS_EPS = 1e-6
LN_EPS = 1e-5
NEG_INF = -1e30
BAND = 64
assert all(w // (2 * d) == BAND for w, d in DILATED_GROUPS)
assert HEADS_PER_GROUP * HEAD_DIM == LANES

OFF_A = 0
OFF_B = 2 * W_A
OFF_C = OFF_B + 3 * W_B
OFF_G = OFF_C + 2 * W_C_QK + W_C
IN_COLS = OFF_G + N_BRANCH * D_MODEL
GRP_COLS = 3 * LANES

FF_CHUNK = 256
DILATED_GROUP_ROWS = 1024
KEY_CHUNK = 256
DIFF_LOGITS_ELEMS = 2 ** 21
ONES_ROWS = 16
LOG2E = math.log2(math.e)
VMEM_LIMIT = 56 * 2 ** 20
DIFF_VMEM_LIMIT = 60 * 2 ** 20

_F32 = jnp.float32
_BF16 = jnp.bfloat16


def _in_proj_permutation():
    cols = list(range(OFF_B))
    for g in range(N_GROUPS):
        for part in range(3):
            base = OFF_B + part * W_B + g * LANES
            cols += list(range(base, base + LANES))
    for part in range(2):
        for h in range(H_C):
            for c in range(2):
                base = OFF_C + part * W_C_QK + c * (H_C * HEAD_DIM) + h * HEAD_DIM
                cols += list(range(base, base + HEAD_DIM))
    cols += list(range(OFF_C + 2 * W_C_QK, IN_COLS))
    assert sorted(cols) == list(range(IN_COLS))
    runs, start = [], 0
    for i in range(1, IN_COLS + 1):
        if i == IN_COLS or cols[i] != cols[i - 1] + 1:
            runs.append((cols[start], cols[i - 1] + 1))
            start = i
    return runs


def _bucket_thresholds(max_dist):
    nb = NUM_BUCKETS // 2
    max_exact = nb // 2
    n = np.arange(1, max_dist + 1)

    def idx(dtype):
        nf = n.astype(dtype)
        large = max_exact + (np.log(nf / dtype(max_exact)) / dtype(math.log(REL_MAX_DIST / max_exact))
                             * dtype(nb - max_exact)).astype(np.int32)
        return np.where(n < max_exact, n, np.minimum(large, nb - 1))

    i32, i64 = idx(np.float32), idx(np.float64)
    assert np.array_equal(i32, i64) and np.all(np.diff(i32) >= 0) and np.all(np.diff(i32) <= 1)
    thr = [0] + [int(n[np.argmax(i32 >= b)]) for b in range(1, nb)]
    return thr


_THR = _bucket_thresholds(1 << 16)


def _bias_from_rel(rel, tab_ref, head):
    nb = NUM_BUCKETS // 2
    val = jnp.full(rel.shape, tab_ref[nb - 1, head], _F32)
    for b in range(nb - 2, -1, -1):
        val = jnp.where(rel >= -(_THR[b + 1] - 1), tab_ref[b, head], val)
    for b in range(1, nb):
        val = jnp.where(rel >= _THR[b], tab_ref[nb + b, head], val)
    return val


def _bias_c_kernel(tab_ref, out_ref, band_scr):
    h = pl.program_id(0)
    j = pl.program_id(1)
    _, s, tc = out_ref.shape

    @pl.when(j == 0)
    def _():
        def fill(i, carry):
            y0 = pl.multiple_of(i * tc, tc)
            y = lax.broadcasted_iota(jnp.int32, (tc, tc), 0) + y0
            c = lax.broadcasted_iota(jnp.int32, (tc, tc), 1)
            band_scr[pl.ds(y0, tc), :] = _bias_from_rel(y - (s - tc) - c, tab_ref, H_B + h) * LOG2E
            return carry

        lax.fori_loop(0, band_scr.shape[0] // tc, fill, 0)

    out_ref[0] = band_scr[pl.ds(pl.multiple_of((s - tc) - j * tc, tc), s), :]


def _bias_c(rel_bias, s):
    tc = min(256, s)
    return pl.pallas_call(
        _bias_c_kernel,
        grid=(H_C, s // tc),
        in_specs=[pl.BlockSpec(memory_space=pltpu.SMEM)],
        out_specs=pl.BlockSpec((1, s, tc), lambda h, j: (h, 0, j)),
        out_shape=jax.ShapeDtypeStruct((H_C, s, s), _F32),
        scratch_shapes=[pltpu.VMEM((2 * s - tc, tc), _F32)],
        compiler_params=pltpu.CompilerParams(dimension_semantics=("arbitrary", "arbitrary")),
        name="bias_c",
    )(rel_bias)


def _band_tiling(length):
    tq = min(256, length)
    kw = min(tq + 2 * BAND, length)
    nt = length // tq
    assert nt * tq == length
    starts = [min(max(t * tq - BAND, 0), length - kw) for t in range(nt)]
    offs = sorted({starts[t] - t * tq for t in range(nt)})
    variants = [offs.index(starts[t] - t * tq) for t in range(nt)]
    return tq, kw, starts, offs, variants


def _bias_b_kernel(tab_ref, out_ref, *, group, dil, offs):
    _, _, tq, kw = out_ref.shape
    i = lax.broadcasted_iota(jnp.int32, (tq, kw), 0)
    j = lax.broadcasted_iota(jnp.int32, (tq, kw), 1)
    for v, off in enumerate(offs):
        dist = j + off - i
        inside = jnp.abs(dist) <= BAND
        for hh in range(HEADS_PER_GROUP):
            val = _bias_from_rel(dist * dil, tab_ref, group * HEADS_PER_GROUP + hh)
            out_ref[hh, v] = jnp.where(inside, val, NEG_INF)


def _bias_b(rel_bias, group, dil, length):
    tq, kw, _, offs, _ = _band_tiling(length)
    return pl.pallas_call(
        functools.partial(_bias_b_kernel, group=group, dil=dil, offs=offs),
        in_specs=[pl.BlockSpec(memory_space=pltpu.SMEM)],
        out_shape=jax.ShapeDtypeStruct((HEADS_PER_GROUP, len(offs), tq, kw), _F32),
        name=f"bias_b{group}",
    )(rel_bias)


def _head_rms(z, gain):
    lane = lax.broadcasted_iota(jnp.int32, (1, LANES), 1)
    first = lane < HEAD_DIM
    sq = z * z
    s0 = jnp.sum(jnp.where(first, sq, 0.0), axis=-1, keepdims=True)
    s1 = jnp.sum(jnp.where(first, 0.0, sq), axis=-1, keepdims=True)
    ms = jnp.where(first, s0, s1) * (1.0 / HEAD_DIM)
    return (z * lax.rsqrt(ms + RMS_EPS)) * gain


def _sigmoid(x):
    return 1.0 / (1.0 + jnp.exp(-x))


def _gelu_tanh(x):
    c = math.sqrt(2.0 / math.pi)
    return x * (0.5 * (1.0 + jnp.tanh(c * (x + 0.044715 * (x * x * x)))))


def _in_proj_kernel(l_ref, x_ref, n1g_ref, win_ref, lng_ref, lnb_ref, sguw_ref, sgub_ref,
                    qgb_ref, kgb_ref, qgc_ref, kgc_ref, wpa_ref,
                    ma_ref, g12_ref, zb_ref, qkc_ref, vct_ref, a_scr):
    del l_ref
    tm = x_ref.shape[0]
    first = lax.broadcasted_iota(jnp.int32, (1, LANES), 1) < HEAD_DIM

    def project(rows):
        x = x_ref[rows, :]
        h = ((x * lax.rsqrt(jnp.mean(x * x, axis=-1, keepdims=True) + RMS_EPS)) * n1g_ref[0]).astype(_BF16)
        proj = lambda lo, hi: jnp.dot(h, win_ref[0, :, lo:hi], preferred_element_type=_F32)
        za = proj(OFF_A, OFF_B)
        gates = _sigmoid(proj(OFF_G, IN_COLS))
        g12_ref[rows, :] = gates[:, D_MODEL:]
        return za, gates[:, :D_MODEL], proj(OFF_B, OFF_C), proj(OFF_C, OFF_G)

    def mix(rows, za, gate0, zb, zc):
        ga = _gelu_tanh(za)
        u, v = ga[:, :W_A], ga[:, W_A:]
        vc = v - jnp.mean(v, axis=-1, keepdims=True)
        var = jnp.mean(vc * vc, axis=-1, keepdims=True)
        vn = ((vc * lax.rsqrt(var + LN_EPS)) * lng_ref[0] + lnb_ref[0]).astype(_BF16)
        for c in range(za.shape[0] // CHUNK):
            chunk = slice(c * CHUNK, (c + 1) * CHUNK)
            dst = slice(rows.start + c * CHUNK, rows.start + (c + 1) * CHUNK)
            for j in range(A_GROUPS // 2):
                cols = slice(j * LANES, (j + 1) * LANES)
                slab = vn[chunk, cols]
                r0 = jnp.dot(sguw_ref[0, 2 * j], slab, preferred_element_type=_F32)
                r1 = jnp.dot(sguw_ref[0, 2 * j + 1], slab, preferred_element_type=_F32)
                sv = jnp.where(first, r0, r1) + sgub_ref[0, :, cols]
                a_scr[dst, cols] = (u[chunk, cols] * sv).astype(_BF16)
        ma_ref[rows, :] = gate0 * jnp.dot(a_scr[rows, :], wpa_ref[0], preferred_element_type=_F32)

        for g in range(N_GROUPS):
            base = g * GRP_COLS
            zb_ref[3 * g, rows, :] = _head_rms(zb[:, base:base + LANES], qgb_ref[0]) * HEAD_DIM ** -0.5
            zb_ref[3 * g + 1, rows, :] = _head_rms(zb[:, base + LANES:base + 2 * LANES], kgb_ref[0])
            zb_ref[3 * g + 2, rows, :] = zb[:, base + 2 * LANES:base + 3 * LANES]

        for hh in range(H_C):
            cols = slice(hh * LANES, (hh + 1) * LANES)
            k_cols = slice(W_C_QK + hh * LANES, W_C_QK + (hh + 1) * LANES)
            qkc_ref[hh, rows, :] = (_head_rms(zc[:, cols], qgc_ref[0]) * (HEAD_DIM ** -0.5 * LOG2E)).astype(_BF16)
            qkc_ref[H_C + hh, rows, :] = _head_rms(zc[:, k_cols], kgc_ref[0]).astype(_BF16)
        vct_ref[:, rows] = zc[:, 2 * W_C_QK:].T.astype(_BF16)

    n_split = 2 if tm % (2 * LANES) == 0 else 1
    parts = [slice(i * tm // n_split, (i + 1) * tm // n_split) for i in range(n_split)]
    projected = [project(rows) for rows in parts]
    for rows, vals in zip(parts, projected):
        mix(rows, *vals)


def _layer_spec(shape):
    nd = len(shape)
    return pl.BlockSpec((1,) + tuple(shape[1:]), lambda i, l: (l[0],) + (0,) * (nd - 1),
                        pipeline_mode=pl.Buffered(1))


def _in_proj(l, x, p):
    t = x.shape[0]
    tm = min(512, t)
    row = lambda w: pl.BlockSpec((tm, w), lambda i, l: (i, 0))
    weights = (p["norm1_g"], p["w_in"], p["sgu_ln_g"], p["sgu_ln_b"], p["sgu_w"], p["sgu_b"],
               p["qn_b"], p["kn_b"], p["qn_c"], p["kn_c"], p["w_pa"])
    slabs = lambda n: pl.BlockSpec((n, tm, LANES), lambda i, l: (0, i, 0))
    return pl.pallas_call(
        _in_proj_kernel,
        grid_spec=pltpu.PrefetchScalarGridSpec(
            num_scalar_prefetch=1,
            grid=(t // tm,),
            in_specs=[row(D_MODEL)] + [_layer_spec(w.shape) for w in weights],
            out_specs=[row(D_MODEL), row(2 * D_MODEL), slabs(3 * N_GROUPS), slabs(2 * H_C),
                       pl.BlockSpec((W_C, tm), lambda i, l: (0, i))],
            scratch_shapes=[pltpu.VMEM((tm, W_A), _BF16)],
        ),
        out_shape=[jax.ShapeDtypeStruct((t, D_MODEL), _F32), jax.ShapeDtypeStruct((t, 2 * D_MODEL), _F32),
                   jax.ShapeDtypeStruct((3 * N_GROUPS, t, LANES), _F32),
                   jax.ShapeDtypeStruct((2 * H_C, t, LANES), _BF16),
                   jax.ShapeDtypeStruct((W_C, t), _BF16)],
        compiler_params=pltpu.CompilerParams(dimension_semantics=("parallel",),
                                             vmem_limit_bytes=VMEM_LIMIT),
        name="in_proj",
    )(l, x, *weights)


def _dilated_kernel(q_ref, k_ref, v_ref, bias_ref, o_ref, lse_ref, *, dil, tq, kw, starts, variants, group):
    lane = lax.broadcasted_iota(jnp.int32, (1, LANES), 1)
    first = lane < HEAD_DIM
    head_mask = (first.astype(_BF16), (~first).astype(_BF16))
    heads = range(HEADS_PER_GROUP)
    tiles = [(r, t, ks, var) for r in range(dil) for t, (ks, var) in enumerate(zip(starts, variants))]
    for g0 in range(0, len(tiles), group):
        probs = tiles[g0:g0 + group]
        rows, qs, ks_, vs = [], [], [], []
        for r, t, ks, _ in probs:
            q_rows = pl.ds(r + t * tq * dil, tq, stride=dil)
            k_rows = pl.ds(r + ks * dil, kw, stride=dil)
            rows.append(q_rows)
            qs.append(q_ref[0, q_rows, :].astype(_BF16))
            ks_.append(k_ref[0, k_rows, :].astype(_BF16))
            vs.append(v_ref[0, k_rows, :].astype(_BF16))
        s = [[lax.dot_general(q * head_mask[hh], k, (((1,), (1,)), ((), ())),
                              preferred_element_type=_F32) + bias_ref[hh, p[3]] for hh in heads]
             for q, k, p in zip(qs, ks_, probs)]
        m = [[jnp.max(x, axis=-1, keepdims=True) for x in sp] for sp in s]
        e = [[jnp.exp(x - mx) for x, mx in zip(sp, mp)] for sp, mp in zip(s, m)]
        den = [[jnp.sum(x, axis=-1, keepdims=True) for x in ep] for ep in e]
        o = [[jnp.dot(x.astype(_BF16), v, preferred_element_type=_F32) * (1.0 / d) for x, d in zip(ep, dp)]
             for ep, dp, v in zip(e, den, vs)]
        for q_rows, op, mp, dp in zip(rows, o, m, den):
            o_ref[0, q_rows, :] = jnp.where(first, op[0], op[1])
            lse_ref[0, q_rows, :] = jnp.where(first, mp[0] + jnp.log(dp[0]), mp[1] + jnp.log(dp[1]))


def _dilated_attention(zb, bias, group, dil, batch, seq):
    tq, kw, starts, _, variants = _band_tiling(seq // dil)
    out = jax.ShapeDtypeStruct((batch, seq, LANES), _F32)
    o, lse = pl.pallas_call(
        functools.partial(_dilated_kernel, dil=dil, tq=tq, kw=kw, starts=starts, variants=variants,
                          group=max(1, DILATED_GROUP_ROWS // tq)),
        grid=(batch,),
        in_specs=[pl.BlockSpec((1, seq, LANES), lambda b: (3 * group, b, 0)),
                  pl.BlockSpec((1, seq, LANES), lambda b: (3 * group + 1, b, 0)),
                  pl.BlockSpec((1, seq, LANES), lambda b: (3 * group + 2, b, 0)),
                  pl.BlockSpec(bias.shape, lambda b: (0, 0, 0, 0))],
        out_specs=[pl.BlockSpec((1, seq, LANES), lambda b: (b, 0, 0))] * 2,
        out_shape=[out, out],
        compiler_params=pltpu.CompilerParams(dimension_semantics=("parallel",),
                                             vmem_limit_bytes=VMEM_LIMIT),
        name=f"dilated_d{dil}",
    )(zb, zb, zb, bias)
    return o.reshape(batch * seq, LANES), lse.reshape(batch * seq, LANES)


def _sublane_group_max(s):
    parts = [s[i:i + 8] for i in range(0, s.shape[0], 8)]
    while len(parts) > 1:
        parts = [jnp.maximum(a, b) for a, b in zip(parts[0::2], parts[1::2])] + parts[len(parts) & ~1:]
    return parts[0]


def _diff_kernel(l_ref, lam_init_ref, q_ref, k_ref, vt_ref, bias_ref, lq1_ref, lk1_ref, lq2_ref, lk2_ref,
                 sg_ref, o_ref, s0_scr, s1_scr, m_scr):
    seq = k_ref.shape[1]
    tq = q_ref.shape[1]
    kc = min(KEY_CHUNK, seq)
    s_scr = (s0_scr, s1_scr)

    @pl.when(pl.program_id(0) == 0)
    def _():
        s0_scr[...] = jnp.zeros_like(s0_scr)
        s1_scr[...] = jnp.zeros_like(s1_scr)
        m_scr[...] = jnp.zeros_like(m_scr)

    lam_init = lam_init_ref[l_ref[0]]
    lam = (jnp.exp(jnp.sum(lq1_ref[0] * lk1_ref[0], axis=-1, keepdims=True))
           - jnp.exp(jnp.sum(lq2_ref[0] * lk2_ref[0], axis=-1, keepdims=True)) + lam_init)
    first = lax.broadcasted_iota(jnp.int32, (1, LANES), 1) < HEAD_DIM
    q = q_ref[0]
    qm = [q * jnp.where(mask, 1.0, 0.0).astype(_BF16) for mask in (first, ~first)]
    ones = jnp.ones((ONES_ROWS, kc), _BF16)
    chunks = [slice(j, j + kc) for j in range(0, seq, kc)]
    m_prev = [m_scr[c, 0:1, :] for c in range(2)]

    def value_chunk(c, rows, acc):
        p = jnp.exp2(s_scr[c][rows, :] - m_prev[c]).astype(_BF16)
        d = jnp.dot(jnp.concatenate([vt_ref[:, rows], ones], axis=0), p, preferred_element_type=_F32)
        return d if acc is None else acc + d

    def logits_chunk(c, rows, m8):
        s = lax.dot_general(k_ref[0, rows, :], qm[c], (((1,), (1,)), ((), ())),
                            preferred_element_type=_F32) + bias_ref[0, rows, :]
        s_scr[c][rows, :] = s
        cm = _sublane_group_max(s)
        return cm if m8 is None else jnp.maximum(m8, cm)

    outs = []
    for c in range(2):
        m8 = None
        accs = [None, None]
        for j, rows in enumerate(chunks):
            accs[j % 2] = value_chunk(c, rows, accs[j % 2])
            m8 = logits_chunk(c, rows, m8)
        m_scr[c] = jnp.broadcast_to(jnp.max(m8, axis=0, keepdims=True), (8, tq))
        acc = accs[0] if accs[1] is None else accs[0] + accs[1]
        outs.append(acc[:LANES] * (1.0 / acc[LANES:LANES + 1]))
    o_t = outs[0] - lam * outs[1]
    y_t = o_t * lax.rsqrt(jnp.mean(o_t * o_t, axis=0, keepdims=True) + RMS_EPS)
    o_ref[0] = ((y_t.T * sg_ref[0]) * (1.0 - lam_init)).astype(o_ref.dtype)


def _diff_attention(l, lam_init, qkc, vct, bias_t, p, batch, seq):
    whole = seq * seq <= 2 * DIFF_LOGITS_ELEMS
    tq = seq if whole else min(seq, DIFF_LOGITS_ELEMS // seq)
    nq = seq // tq
    n_tiles = H_C * nq * batch

    def tile(n):
        return n // (nq * batch), (n // batch) % nq, n % batch

    cur = lambda n: tile(jnp.minimum(n, n_tiles - 1))
    prev = lambda n: tile(jnp.maximum(n - 1, 0))

    def q_map(n, l):
        h, i, b = cur(n)
        return h, b * nq + i, 0

    def k_map(n, l):
        h, _, b = cur(n)
        return H_C + h, b, 0

    def bias_map(n, l):
        h, i, _ = cur(n)
        return h, 0, i

    def vt_map(n, l):
        h, _, b = prev(n)
        return h, b

    def out_map(n, l):
        h, i, b = prev(n)
        return h, b * nq + i, 0

    vec = lambda w: pl.BlockSpec((1, 1, w), lambda n, l: (l[0], 0, 0))
    return pl.pallas_call(
        _diff_kernel,
        grid_spec=pltpu.PrefetchScalarGridSpec(
            num_scalar_prefetch=1,
            grid=(n_tiles + 1,),
            in_specs=[pl.BlockSpec(memory_space=pltpu.SMEM),
                      pl.BlockSpec((1, tq, LANES), q_map),
                      pl.BlockSpec((1, seq, LANES), k_map),
                      pl.BlockSpec((LANES, seq), vt_map),
                      pl.BlockSpec((1, seq, tq), bias_map, pipeline_mode=pl.Buffered(1 if whole else 2)),
                      vec(HEAD_DIM), vec(HEAD_DIM), vec(HEAD_DIM), vec(HEAD_DIM), vec(LANES)],
            out_specs=pl.BlockSpec((1, tq, LANES), out_map),
            scratch_shapes=[pltpu.VMEM((seq, tq), _F32), pltpu.VMEM((seq, tq), _F32),
                            pltpu.VMEM((2, 8, tq), _F32)],
        ),
        out_shape=jax.ShapeDtypeStruct((H_C, batch * seq, LANES), _BF16),
        compiler_params=pltpu.CompilerParams(dimension_semantics=("arbitrary",),
                                             vmem_limit_bytes=DIFF_VMEM_LIMIT),
        name="diff_attn",
    )(l, lam_init, qkc, qkc, vct, bias_t, p["lam_q1"], p["lam_k1"], p["lam_q2"], p["lam_k2"], p["subln_g"])


def _merge_ffn_kernel(l_ref, x_ref, ma_ref, g12_ref, o0_ref, o1_ref, o2_ref, s0_ref, s1_ref, s2_ref, c_ref,
                      wpb_ref, wpc_ref, wo_ref, n2g_ref, wgu_ref, wdn_ref, out_ref):
    del l_ref
    lses = (s0_ref[...], s1_ref[...], s2_ref[...])
    m = jnp.maximum(jnp.maximum(lses[0], lses[1]), lses[2])
    es = [jnp.exp(s - m) for s in lses]
    inv = 1.0 / (es[0] + es[1] + es[2])
    b_out = jnp.concatenate([o[...] * (e * inv) for o, e in zip((o0_ref, o1_ref, o2_ref), es)], axis=-1)
    pb = jnp.dot(b_out.astype(_BF16), wpb_ref[0], preferred_element_type=_F32)
    c_out = jnp.concatenate([c_ref[hh] for hh in range(H_C)], axis=-1)
    pc = jnp.dot(c_out, wpc_ref[0], preferred_element_type=_F32)
    g12 = g12_ref[...]
    merged = ma_ref[...] + g12[:, :D_MODEL] * pb + g12[:, D_MODEL:] * pc
    x1 = x_ref[...] + jnp.dot(merged.astype(_BF16), wo_ref[0], preferred_element_type=_F32)

    h2 = ((x1 * lax.rsqrt(jnp.mean(x1 * x1, axis=-1, keepdims=True) + RMS_EPS)) * n2g_ref[0]).astype(_BF16)
    acc = x1
    for j in range(D_FF // FF_CHUNK):
        lo = j * FF_CHUNK
        g = jnp.dot(h2, wgu_ref[0, :, lo:lo + FF_CHUNK], preferred_element_type=_F32)
        u = jnp.dot(h2, wgu_ref[0, :, D_FF + lo:D_FF + lo + FF_CHUNK], preferred_element_type=_F32)
        act = ((g * _sigmoid(g)) * u).astype(_BF16)
        acc = acc + jnp.dot(act, wdn_ref[0, lo:lo + FF_CHUNK, :], preferred_element_type=_F32)
    out_ref[...] = acc


def _merge_ffn(l, x, ma, g12, o_groups, lse_groups, c_out, p, in_place):
    t = x.shape[0]
    tm = min(512, t)
    row = lambda w: pl.BlockSpec((tm, w), lambda i, l: (i, 0))
    weights = (p["w_pb"], p["w_pc"], p["w_o"], p["norm2_g"], p["w_gu"], p["w_down"])
    acts = (x, ma, g12) + tuple(o_groups) + tuple(lse_groups)
    return pl.pallas_call(
        _merge_ffn_kernel,
        input_output_aliases={1: 0} if in_place else {},
        grid_spec=pltpu.PrefetchScalarGridSpec(
            num_scalar_prefetch=1,
            grid=(t // tm,),
            in_specs=[row(a.shape[1]) for a in acts]
            + [pl.BlockSpec((H_C, tm, LANES), lambda i, l: (0, i, 0))]
            + [_layer_spec(w.shape) for w in weights],
            out_specs=row(D_MODEL),
        ),
        out_shape=jax.ShapeDtypeStruct((t, D_MODEL), _F32),
        compiler_params=pltpu.CompilerParams(dimension_semantics=("parallel",),
                                             vmem_limit_bytes=VMEM_LIMIT),
        name="merge_ffn",
    )(l, *acts, c_out, *weights)


def _layer(l, x, p, biases, lam_init, batch, seq, in_place):
    ma, g12, zb, qkc, vct = _in_proj(l, x, p)
    o_groups, lse_groups = [], []
    for g, (_, dil) in enumerate(DILATED_GROUPS):
        o, lse = _dilated_attention(zb, biases["b"][g], g, dil, batch, seq)
        o_groups.append(o)
        lse_groups.append(lse)
    c_out = _diff_attention(l, lam_init, qkc, vct, biases["c"], p, batch, seq)
    return _merge_ffn(l, x, ma, g12, o_groups, lse_groups, c_out, p, in_place)


def _prepare_params(p):
    depth = p["w_in"].shape[0]
    row = lambda a: a.reshape(depth, 1, a.shape[-1])
    tile2 = lambda a: row(jnp.tile(a, (1, LANES // HEAD_DIM)))
    sgu_b = jnp.broadcast_to(jnp.swapaxes(p["sgu_b"], 1, 2)[..., None], (depth, CHUNK, A_GROUPS, HEAD_DIM))
    return dict(
        norm1_g=row(p["norm1_g"]), norm2_g=row(p["norm2_g"]),
        w_in=jnp.concatenate([p["w_in"][:, :, a:b].astype(_BF16) for a, b in _in_proj_permutation()], axis=2),
        sgu_ln_g=row(p["sgu_ln_g"]), sgu_ln_b=row(p["sgu_ln_b"]),
        sgu_w=p["sgu_w"].astype(_BF16), sgu_b=sgu_b.reshape(depth, CHUNK, W_A),
        qn_b=tile2(p["qn_b"]), kn_b=tile2(p["kn_b"]), qn_c=tile2(p["qn_c"]), kn_c=tile2(p["kn_c"]),
        lam_q1=row(p["lam_q1"]), lam_k1=row(p["lam_k1"]), lam_q2=row(p["lam_q2"]), lam_k2=row(p["lam_k2"]),
        subln_g=row(p["subln_g"]),
        w_pa=p["w_pa"].astype(_BF16), w_pb=p["w_pb"].astype(_BF16), w_pc=p["w_pc"].astype(_BF16),
        w_o=p["w_o"].astype(_BF16), w_gu=p["w_gu"].astype(_BF16), w_down=p["w_down"].astype(_BF16),
    )


def _biases(rel_bias, seq):
    return dict(c=_bias_c(rel_bias, seq),
                b=[_bias_b(rel_bias, g, dil, seq // dil) for g, (_, dil) in enumerate(DILATED_GROUPS)])


@jax.jit
def _trunk(x_prompt, x_sample, params):
    p = _prepare_params(params)
    depth = params["w_in"].shape[0]
    lam_init = jnp.asarray([0.8 - 0.6 * math.exp(-0.3 * l) for l in range(depth)], _F32)
    groups = [(x.shape[0], x.shape[1]) for x in (x_prompt, x_sample)]
    biases = [_biases(params["rel_bias"], seq) for _, seq in groups]

    def layer(l, xs, in_place):
        lidx = jnp.reshape(l, (1,)).astype(jnp.int32)
        return tuple(_layer(lidx, x, p, b, lam_init, batch, seq, in_place)
                     for x, b, (batch, seq) in zip(xs, biases, groups))

    xs = tuple(x.reshape(-1, D_MODEL) for x in (x_prompt, x_sample))
    xs = layer(jnp.int32(0), xs, in_place=False)
    ys = lax.fori_loop(1, depth, functools.partial(layer, in_place=True), xs)
    return tuple(y.reshape(x.shape) for y, x in zip(ys, (x_prompt, x_sample)))


def kernel(x_prompt, x_sample, rel_bias, norm1_g, w_in, sgu_ln_g, sgu_ln_b, sgu_w, sgu_b, qn_b, kn_b, qn_c, kn_c, lam_q1, lam_k1, lam_q2, lam_k2, subln_g, w_pa, w_pb, w_pc, w_o, norm2_g, w_gu, w_down):
    params = dict(rel_bias=rel_bias, norm1_g=norm1_g, w_in=w_in, sgu_ln_g=sgu_ln_g, sgu_ln_b=sgu_ln_b,
                  sgu_w=sgu_w, sgu_b=sgu_b, qn_b=qn_b, kn_b=kn_b, qn_c=qn_c, kn_c=kn_c,
                  lam_q1=lam_q1, lam_k1=lam_k1, lam_q2=lam_q2, lam_k2=lam_k2, subln_g=subln_g,
                  w_pa=w_pa, w_pb=w_pb, w_pc=w_pc, w_o=w_o, norm2_g=norm2_g, w_gu=w_gu, w_down=w_down)
    return _trunk(x_prompt, x_sample, params)
```
